```python
import math
import jax, jax.numpy as jnp
from jax import lax
import numpy as np

D_MODEL = 1024
BATCH = 2
SEQ = 8192
DEPTH = 4
DEC_BATCH = 128
DEC_SEQ = 1
PAST_LEN = 2048
PAGE_SIZE = 128

HEAD_DIM = 64
A_HEADS = (D_MODEL // 2) // HEAD_DIM
A_WIDTH = A_HEADS * HEAD_DIM
IDX_HEADS = 8
IDX_DIM = 64
TOPK_MAX = 256
Q_BLOCK = 128
B_GDIM = 64
B_GROUPS = (D_MODEL // 2) // B_GDIM
B_WIDTH = B_GROUPS * B_GDIM
CHUNK = 128
C_HDIM = 64
C_HEADS = D_MODEL // C_HDIM
DECAY_LORA = 64
AAA_LORA = 64
MV_LORA = 32
N_EVEN = (DEPTH + 1) // 2
N_ODD = DEPTH // 2
N_VRES = max(N_ODD - 1, 0)
ALPHA = (2 * DEPTH) ** 0.25
BETA = (8 * DEPTH) ** -0.25
LN_EPS = 1e-5
GN_EPS = 64e-5
EVEN_SIZES = (A_WIDTH, A_WIDTH, A_WIDTH, IDX_HEADS * IDX_DIM, IDX_DIM, IDX_HEADS, A_WIDTH, B_WIDTH, B_WIDTH, B_WIDTH)
EVEN_IN = sum(EVEN_SIZES)
EVEN_SPLIT = tuple(sum(EVEN_SIZES[: i + 1]) for i in range(len(EVEN_SIZES) - 1))

kernel_name = "dsa_gmlp_rwkv7_deepnorm_hybrid_step"

F32 = jnp.float32


def layer_norm(x, g, b, eps=LN_EPS):
    xf = x.astype(F32)
    mu = jnp.mean(xf, axis=-1, keepdims=True)
    var = jnp.mean(jnp.square(xf - mu), axis=-1, keepdims=True)
    return ((xf - mu) * lax.rsqrt(var + eps) * g.astype(F32) + b.astype(F32)).astype(x.dtype)


def indexer_topk(qi, wi, ki, qpos, topk):
    sc = jnp.einsum('bthd,bsd->bths', qi.astype(F32), ki.astype(F32)) * (IDX_DIM ** -0.5)
    score = jnp.einsum('bths,bth->bts', jax.nn.relu(sc), wi.astype(F32))
    kpos = jnp.arange(ki.shape[1])
    score = jnp.where(kpos[None, None, :] <= qpos[None, :, None], score, -jnp.inf)
    _, idx = lax.top_k(score, topk)
    return idx


def sparse_attend(q, k_sel, v_sel, pos_sel, qpos, slopes):
    s = jnp.einsum('bthd,btkhd->bhtk', q.astype(F32), k_sel.astype(F32)) * (HEAD_DIM ** -0.5)
    dist = (qpos[None, :, None] - pos_sel).astype(F32)
    s = s - slopes[None, :, None, None] * dist[:, None]
    valid = pos_sel <= qpos[None, :, None]
    s = jnp.where(valid[:, None], s, -jnp.inf)
    p = jax.nn.softmax(s, axis=-1)
    o = jnp.einsum('bhtk,btkhd->bthd', p, v_sel.astype(F32))
    return o.astype(q.dtype)


def mixer_a_prompt(q, k, v, qi, ki, wi, slopes):
    bsz, S = q.shape[:2]
    topk = min(TOPK_MAX, S // 4)
    nblk = S // Q_BLOCK

    def to_blocks(a):
        return jnp.swapaxes(a.reshape(bsz, nblk, Q_BLOCK, *a.shape[2:]), 0, 1)

    def block(xs):
        qb, qib, wib, qpos = xs
        idx = indexer_topk(qib, wib, ki, qpos, topk)
        k_sel = jax.vmap(lambda kb, ib: kb[ib])(k, idx)
        v_sel = jax.vmap(lambda vb, ib: vb[ib])(v, idx)
        return sparse_attend(qb, k_sel, v_sel, idx, qpos, slopes)

    qpos = jnp.arange(S).reshape(nblk, Q_BLOCK)
    o = lax.map(block, (to_blocks(q), to_blocks(qi), to_blocks(wi), qpos))
    return jnp.swapaxes(o, 0, 1).reshape(bsz, S, A_HEADS, HEAD_DIM)


def mixer_a_sample(q, k_new, v_new, qi, ki_new, wi, cache_k, cache_v, cache_idx_k, page_table, e, slopes):
    db, T = q.shape[:2]
    n_pages = page_table.shape[1]
    past = n_pages * PAGE_SIZE
    topk = min(TOPK_MAX, (past + T) // 4)
    ik_past = cache_idx_k[e, page_table].reshape(db, past, IDX_DIM)
    ik_all = jnp.concatenate([ik_past.astype(ki_new.dtype), ki_new], axis=1)
    qpos = past + jnp.arange(T)
    idx = indexer_topk(qi, wi, ik_all, qpos, topk)
    in_past = idx < past
    pc = jnp.minimum(idx, past - 1)
    phys = jax.vmap(lambda pt, p: pt[p])(page_table, pc // PAGE_SIZE)
    off = pc % PAGE_SIZE
    nc = jnp.clip(idx - past, 0, T - 1)
    k_cur = jax.vmap(lambda kb, ib: kb[ib])(k_new, nc)
    v_cur = jax.vmap(lambda vb, ib: vb[ib])(v_new, nc)
    k_sel = jnp.where(in_past[..., None, None], cache_k[e, phys, off].astype(k_new.dtype), k_cur)
    v_sel = jnp.where(in_past[..., None, None], cache_v[e, phys, off].astype(v_new.dtype), v_cur)
    return sparse_attend(q, k_sel, v_sel, idx, qpos, slopes)


def mixer_b(u, v, w_s, b_s, ln_g, ln_b):
    bsz, T = u.shape[:2]
    vn = layer_norm(v, ln_g, ln_b)
    pad = (-T) % CHUNK
    nch = (T + pad) // CHUNK
    vp = jnp.pad(vn, ((0, 0), (0, pad), (0, 0))).reshape(bsz, nch, CHUNK, B_GROUPS, B_GDIM)
    mask = jnp.tril(jnp.ones((CHUNK, CHUNK), dtype=bool))
    ws = jnp.where(mask[None], w_s, jnp.zeros_like(w_s))
    mixed = jnp.einsum('gij,bcjgd->bcigd', ws, vp) + b_s.T[None, None, :, :, None]
    mixed = mixed.reshape(bsz, nch * CHUNK, B_WIDTH)[:, :T]
    return u * mixed.astype(u.dtype), vn


def rwkv7_mix(x, shift_prev, wkv0, mu, w_in, w0, w1, w2, a0, a1, a2, k_k, k_a, r_k,
              lnx_g, lnx_b, w_out, v_first, v_res):
    bsz, T, D = x.shape
    x_prev = jnp.concatenate([shift_prev[:, None].astype(x.dtype), x[:, :-1]], axis=1)
    xx = x_prev - x
    xm = x[None] + xx[None] * mu[:, None, None, :]
    xr, xw, xk, xv, xa, xg = xm[0], xm[1], xm[2], xm[3], xm[4], xm[5]
    rkvg = jnp.einsum('jbtd,jde->jbte', jnp.stack([xr, xk, xv, xg]), w_in)
    r, k, v, g = rkvg[0], rkvg[1], rkvg[2], rkvg[3]
    w_log = -jax.nn.softplus(-(w0 + jnp.tanh(xw @ w1) @ w2)) - 0.5
    decay = jnp.exp(-jnp.exp(w_log.astype(F32)))
    a = jax.nn.sigmoid(a0 + (xa @ a1) @ a2)
    if v_res is None:
        v_first = v
    else:
        v0, v1, v2 = v_res
        v = v + (v_first - v) * jax.nn.sigmoid(v0 + (xv @ v1) @ v2)

    def heads(t):
        return t.reshape(bsz, T, C_HEADS, C_HDIM).astype(F32)

    kk = heads(k * k_k)
    kk = kk / jnp.maximum(jnp.sqrt(jnp.sum(kk * kk, axis=-1, keepdims=True)), 1e-12)
    k = k * (1.0 + (a - 1.0) * k_a)
    rh, kh, vh, ah, wh = heads(r), heads(k), heads(v), heads(a), heads(decay)

    def step(S, inp):
        r_t, w_t, k_t, v_t, kk_t, a_t = inp
        sa = jnp.einsum('bhvk,bhk->bhv', S, -kk_t)
        S = S * w_t[:, :, None, :] + sa[..., None] * (kk_t * a_t)[:, :, None, :] + v_t[..., None] * k_t[:, :, None, :]
        return S, jnp.einsum('bhvk,bhk->bhv', S, r_t)

    tm = lambda t: jnp.swapaxes(t, 0, 1)
    S_fin, out = lax.scan(step, wkv0.astype(F32), (tm(rh), tm(wh), tm(kh), tm(vh), tm(kk), tm(ah)))
    out = tm(out)
    m = jnp.mean(out, axis=-1, keepdims=True)
    var = jnp.mean(jnp.square(out - m), axis=-1, keepdims=True)
    out = ((out - m) * lax.rsqrt(var + GN_EPS)).reshape(bsz, T, D) * lnx_g.astype(F32) + lnx_b.astype(F32)
    bonus = jnp.sum(rh * kh * r_k.astype(F32), axis=-1, keepdims=True) * vh
    out = (out + bonus.reshape(bsz, T, D)).astype(x.dtype) * jax.nn.silu(g)
    return out @ w_out, S_fin.astype(wkv0.dtype), x[:, -1], v_first


def setup_inputs(seed: int = 0) -> dict:
    key = jax.random.key(seed)
    ks = iter(jax.random.split(key, 48))
    n_pages = PAST_LEN // PAGE_SIZE
    n_pool = (DEC_BATCH * n_pages * 5) // 4

    def nrm(shape, scale):
        return jax.random.normal(next(ks), shape, F32) * scale

    def gain(shape):
        return 1.0 + nrm(shape, 0.01)

    d_is = D_MODEL ** -0.5
    inp = {}
    inp['x_prompt'] = nrm((BATCH, SEQ, D_MODEL), 1.0)
    inp['x_sample'] = nrm((DEC_BATCH, DEC_SEQ, D_MODEL), 1.0)
    inp['cache_k'] = nrm((N_EVEN, n_pool, PAGE_SIZE, A_HEADS, HEAD_DIM), 1.0)
    inp['cache_v'] = nrm((N_EVEN, n_pool, PAGE_SIZE, A_HEADS, HEAD_DIM), 1.0)
    inp['cache_idx_k'] = nrm((N_EVEN, n_pool, PAGE_SIZE, IDX_DIM), 1.0)
    inp['state_wkv'] = nrm((N_ODD, DEC_BATCH, C_HEADS, C_HDIM, C_HDIM), 0.1)
    inp['state_shift'] = nrm((N_ODD, DEC_BATCH, D_MODEL), 1.0)
    inp['page_table'] = jax.random.permutation(next(ks), n_pool)[: DEC_BATCH * n_pages].reshape(DEC_BATCH, n_pages).astype(jnp.int32)
    inp['ln_g'] = gain((DEPTH, D_MODEL))
    inp['ln_b'] = nrm((DEPTH, D_MODEL), 0.01)
    inp['e_w_in'] = nrm((N_EVEN, D_MODEL, EVEN_IN), d_is)
    inp['e_w_out'] = nrm((N_EVEN, A_WIDTH + B_WIDTH, D_MODEL), (A_WIDTH + B_WIDTH) ** -0.5 * BETA)
    inp['b_ws'] = nrm((N_EVEN, B_GROUPS, CHUNK, CHUNK), CHUNK ** -0.5)
    inp['b_bs'] = 1.0 + nrm((N_EVEN, B_GROUPS, CHUNK), 0.1)
    inp['b_ln_g'] = gain((N_EVEN, B_WIDTH))
    inp['b_ln_b'] = nrm((N_EVEN, B_WIDTH), 0.01)
    inp['c_mu'] = jax.random.uniform(next(ks), (N_ODD, 6, D_MODEL), F32)
    inp['c_w_in'] = nrm((N_ODD, 4, D_MODEL, D_MODEL), d_is)
    inp['c_w0'] = -1.0 + nrm((N_ODD, D_MODEL), 0.5)
    inp['c_w1'] = nrm((N_ODD, D_MODEL, DECAY_LORA), d_is)
    inp['c_w2'] = nrm((N_ODD, DECAY_LORA, D_MODEL), 0.1 * DECAY_LORA ** -0.5)
    inp['c_a0'] = nrm((N_ODD, D_MODEL), 0.1)
    inp['c_a1'] = nrm((N_ODD, D_MODEL, AAA_LORA), d_is)
    inp['c_a2'] = nrm((N_ODD, AAA_LORA, D_MODEL), 0.1 * AAA_LORA ** -0.5)
    inp['c_v0'] = 1.0 + nrm((N_VRES, D_MODEL), 0.1)
    inp['c_v1'] = nrm((N_VRES, D_MODEL, MV_LORA), d_is)
    inp['c_v2'] = nrm((N_VRES, MV_LORA, D_MODEL), 0.1 * MV_LORA ** -0.5)
    inp['c_kk'] = 0.85 + nrm((N_ODD, D_MODEL), 0.05)
    inp['c_ka'] = 1.0 + nrm((N_ODD, D_MODEL), 0.05)
    inp['c_rk'] = nrm((N_ODD, C_HEADS, C_HDIM), 0.1)
    inp['c_lnx_g'] = gain((N_ODD, D_MODEL))
    inp['c_lnx_b'] = nrm((N_ODD, D_MODEL), 0.01)
    inp['c_w_out'] = nrm((N_ODD, D_MODEL, D_MODEL), d_is * BETA)
    return inp


def reference(x_prompt, x_sample, cache_k, cache_v, cache_idx_k, state_wkv, state_shift, page_table,
              ln_g, ln_b, e_w_in, e_w_out, b_ws, b_bs, b_ln_g, b_ln_b,
              c_mu, c_w_in, c_w0, c_w1, c_w2, c_a0, c_a1, c_a2, c_v0, c_v1, c_v2,
              c_kk, c_ka, c_rk, c_lnx_g, c_lnx_b, c_w_out):
    slopes = jnp.exp2(-8.0 * (jnp.arange(A_HEADS, dtype=F32) + 1.0) / A_HEADS)

    def run(x, attend, wkv0, shift0):
        ks_, vs_, iks_, bvs_, wkvs_, shs_ = [], [], [], [], [], []
        v_first = None
        bsz, T = x.shape[:2]
        n_open = (T - 1) % CHUNK + 1
        for l in range(DEPTH):
            if l % 2 == 0:
                e = l // 2
                h = jnp.einsum('btd,de->bte', x, e_w_in[e])
                q, k, v, qi, ki, wi, ga, ub, vb, gb = jnp.split(h, EVEN_SPLIT, axis=-1)
                q = q.reshape(bsz, T, A_HEADS, HEAD_DIM)
                k = k.reshape(bsz, T, A_HEADS, HEAD_DIM)
                v = v.reshape(bsz, T, A_HEADS, HEAD_DIM)
                qi = qi.reshape(bsz, T, IDX_HEADS, IDX_DIM)
                wi = wi * (IDX_HEADS ** -0.5)
                oa = attend(e, q, k, v, qi, ki, wi).reshape(bsz, T, A_WIDTH) * jax.nn.silu(ga)
                ob, vn = mixer_b(ub, vb, b_ws[e], b_bs[e], b_ln_g[e], b_ln_b[e])
                ob = ob * jax.nn.silu(gb)
                out = jnp.concatenate([oa, ob], axis=-1) @ e_w_out[e]
                ks_.append(k); vs_.append(v); iks_.append(ki); bvs_.append(vn[:, T - n_open:])
            else:
                o = l // 2
                v_res = None if o == 0 else (c_v0[o - 1], c_v1[o - 1], c_v2[o - 1])
                out, S_fin, sh, v_first = rwkv7_mix(
                    x, shift0[o], wkv0[o], c_mu[o], c_w_in[o], c_w0[o], c_w1[o], c_w2[o],
                    c_a0[o], c_a1[o], c_a2[o], c_kk[o], c_ka[o], c_rk[o],
                    c_lnx_g[o], c_lnx_b[o], c_w_out[o], v_first, v_res)
                wkvs_.append(S_fin); shs_.append(sh)
            x = layer_norm(ALPHA * x + out, ln_g[l], ln_b[l])
        return (x, jnp.stack(ks_), jnp.stack(vs_), jnp.stack(iks_), jnp.stack(bvs_),
                jnp.stack(wkvs_), jnp.stack(shs_))

    def attend_prompt(e, q, k, v, qi, ki, wi):
        return mixer_a_prompt(q, k, v, qi, ki, wi, slopes)

    def attend_sample(e, q, k, v, qi, ki, wi):
        return mixer_a_sample(q, k, v, qi, ki, wi, cache_k, cache_v, cache_idx_k, page_table, e, slopes)

    bp = x_prompt.shape[0]
    wkv_p0 = jnp.zeros((N_ODD, bp, C_HEADS, C_HDIM, C_HDIM), x_prompt.dtype)
    shift_p0 = jnp.zeros((N_ODD, bp, D_MODEL), x_prompt.dtype)
    y_prompt, p_k, p_v, p_ik, p_bv, p_wkv, p_shift = run(x_prompt, attend_prompt, wkv_p0, shift_p0)
    y_sample, s_k, s_v, s_ik, s_bv, s_wkv, s_shift = run(x_sample, attend_sample, state_wkv, state_shift)
    return (y_prompt, y_sample, p_k, p_v, p_ik, p_bv, p_wkv, p_shift, s_k, s_v, s_ik, s_bv, s_wkv, s_shift)
```

```python
import functools

import jax
import jax.numpy as jnp
from jax import lax
from jax.experimental import pallas as pl
from jax.experimental.pallas import tpu as pltpu

F32, BF16, I32 = jnp.float32, jnp.bfloat16, jnp.int32

D_MODEL = 1024
DEPTH = 4
HEAD_DIM = 64
A_HEADS = 8
A_WIDTH = A_HEADS * HEAD_DIM
IDX_HEADS = 8
IDX_DIM = 64
TOPK_MAX = 256
B_GROUPS = 8
B_GDIM = 64
B_WIDTH = B_GROUPS * B_GDIM
CHUNK = 128
C_HEADS = 16
C_HDIM = 64
PAGE_SIZE = 128
ALPHA = (2 * DEPTH) ** 0.25
LN_EPS = 1e-5
GN_EPS = 64e-5
_OFF_QKVQI = 0
_OFF_KI = 2048
_OFF_WI = 2112
_OFF_REST = 2120
ALIBI_SLOPES = tuple(2.0 ** (-8.0 * (h + 1.0) / A_HEADS) for h in range(A_HEADS))

LANES = 128
SUBLANES = 8
V7X_VMEM_BYTES = 64 * 1024 * 1024

QT = 128
KT = 128
RW_CHUNK = 64
PAIR = 2 * C_HDIM
N_PAIRS = C_HEADS // 2
INT_MIN = -2 ** 31
NEG_BIG = -1e30


def _dot(a, b):
    return jnp.dot(a, b, preferred_element_type=F32)


def _dot_nt(a, b):
    return lax.dot_general(a, b, (((1,), (1,)), ((), ())), preferred_element_type=F32)


def _hdot(a, b):
    return jnp.dot(a, b, precision=lax.Precision.HIGHEST, preferred_element_type=F32)


def _hdot_nt(a, b):
    return lax.dot_general(a, b, (((1,), (1,)), ((), ())), precision=lax.Precision.HIGHEST,
                           preferred_element_type=F32)


def _hdot_tn(a, b):
    return lax.dot_general(a, b, (((0,), (0,)), ((), ())), precision=lax.Precision.HIGHEST,
                           preferred_element_type=F32)


def _dot_exact_rhs(x, a_bf16):
    hi = x.astype(BF16)
    r1 = x - hi.astype(F32)
    mid = r1.astype(BF16)
    lo = (r1 - mid.astype(F32)).astype(BF16)
    return _dot(hi, a_bf16) + _dot(mid, a_bf16) + _dot(lo, a_bf16)


def _layer_norm(x, g, b, eps=LN_EPS):
    mu = jnp.mean(x, axis=-1, keepdims=True)
    d = x - mu
    var = jnp.mean(d * d, axis=-1, keepdims=True)
    return d * lax.rsqrt(var + eps) * g + b


def _sigmoid(x):
    return 1.0 / (1.0 + jnp.exp(-x))


def _silu(x):
    return x * _sigmoid(x)


def _params(semantics, vmem_bytes):
    assert vmem_bytes <= V7X_VMEM_BYTES
    return pltpu.CompilerParams(dimension_semantics=semantics, vmem_limit_bytes=vmem_bytes)


def _even_proj_kernel(x_ref, wa_ref, wki_ref, wwit_ref, wvt_ref, wr_ref,
                      qs_ref, k_ref, kb_ref, v_ref, vt_ref, qi_ref, ki_ref, kib_ref, wit_ref, rest_ref):
    x = x_ref[...].astype(BF16)
    h = _dot(x, wa_ref[...])
    qs_ref[...] = (h[:, 0:A_WIDTH] * (HEAD_DIM ** -0.5)).astype(BF16)
    k = h[:, A_WIDTH:2 * A_WIDTH]
    k_ref[...] = k
    kb_ref[...] = k.astype(BF16)
    v_ref[...] = h[:, 2 * A_WIDTH:3 * A_WIDTH]
    qi_ref[...] = (h[:, 3 * A_WIDTH:4 * A_WIDTH] * (IDX_DIM ** -0.5)).astype(BF16)
    ki = _dot(x, wki_ref[...])
    ki_ref[...] = ki
    kib_ref[...] = ki.astype(BF16)
    wit_ref[0] = _dot_nt(wwit_ref[...], x) * (IDX_HEADS ** -0.5)
    vt = _dot_nt(wvt_ref[...], x)
    for j in range(vt_ref.shape[1]):
        vt_ref[0, j] = vt[:, j * KT:(j + 1) * KT].astype(BF16)
    rest_ref[...] = _dot(x, wr_ref[...])


def _even_proj(x2d, w_in, nb):
    n = x2d.shape[0]
    t = n // nb
    tm = min(256, t)
    tpb = t // tm
    wa = w_in[:, _OFF_QKVQI:_OFF_KI].astype(BF16)
    wki = w_in[:, _OFF_KI:_OFF_WI].astype(BF16)
    wwit = w_in[:, _OFF_WI:_OFF_REST].T.astype(BF16)
    wvt = w_in[:, 2 * A_WIDTH:3 * A_WIDTH].T.astype(BF16)
    wr = w_in[:, _OFF_REST:].astype(BF16)
    row = lambda i: (i, 0)
    const = lambda i: (0, 0)
    out_shape = (
        jax.ShapeDtypeStruct((n, A_WIDTH), BF16),
        jax.ShapeDtypeStruct((n, A_WIDTH), F32),
        jax.ShapeDtypeStruct((n, A_WIDTH), BF16),
        jax.ShapeDtypeStruct((n, A_WIDTH), F32),
        jax.ShapeDtypeStruct((nb, t // KT, A_WIDTH, KT), BF16),
        jax.ShapeDtypeStruct((n, IDX_HEADS * IDX_DIM), BF16),
        jax.ShapeDtypeStruct((n, IDX_DIM), F32),
        jax.ShapeDtypeStruct((n, IDX_DIM), BF16),
        jax.ShapeDtypeStruct((nb, IDX_HEADS, t), F32),
        jax.ShapeDtypeStruct((n, 4 * A_WIDTH), F32),
    )
    out_specs = (
        pl.BlockSpec((tm, A_WIDTH), row),
        pl.BlockSpec((tm, A_WIDTH), row),
        pl.BlockSpec((tm, A_WIDTH), row),
        pl.BlockSpec((tm, A_WIDTH), row),
        pl.BlockSpec((1, tm // KT, A_WIDTH, KT), lambda i: (i // tpb, i % tpb, 0, 0)),
        pl.BlockSpec((tm, IDX_HEADS * IDX_DIM), row),
        pl.BlockSpec((tm, IDX_DIM), row),
        pl.BlockSpec((tm, IDX_DIM), row),
        pl.BlockSpec((1, IDX_HEADS, tm), lambda i: (i // tpb, 0, i % tpb)),
        pl.BlockSpec((tm, 4 * A_WIDTH), row),
    )
    in_specs = [
        pl.BlockSpec((tm, D_MODEL), row),
        pl.BlockSpec(wa.shape, const),
        pl.BlockSpec(wki.shape, const),
        pl.BlockSpec(wwit.shape, const),
        pl.BlockSpec(wvt.shape, const),
        pl.BlockSpec(wr.shape, const),
    ]
    return pl.pallas_call(
        _even_proj_kernel, grid=(n // tm,), in_specs=in_specs, out_specs=out_specs, out_shape=out_shape,
        compiler_params=_params(("arbitrary",), 48 * 1024 * 1024), name="even_proj",
    )(x2d, wa, wki, wwit, wvt, wr)


def _score_keys(score):
    bits = pltpu.bitcast(score + 0.0, I32)
    return jnp.where(bits < 0, bits ^ jnp.int32(0x7FFFFFFF), bits)


def _count(keys_ref, nkb, pred):
    row_i = lax.broadcasted_iota(I32, (KT, LANES), 0)

    def body(kb, acc):
        start = pl.multiple_of(kb * KT, KT)
        blk = keys_ref[pl.ds(start, KT), :]
        hit = jnp.where(pred(blk, row_i + start), 1, 0).astype(I32)
        return acc + jnp.sum(hit.reshape(KT // SUBLANES, SUBLANES, LANES), axis=0)

    acc = lax.fori_loop(0, nkb, body, jnp.zeros((SUBLANES, LANES), I32))
    return jnp.sum(acc, axis=0, keepdims=True)


def _select_threshold(keys_ref, nkb, topk):
    def bit_body(j, thr):
        cand = thr + lax.shift_left(jnp.int32(1), 31 - j)
        cnt = _count(keys_ref, nkb, lambda blk, _: blk >= cand)
        return jnp.where(cnt >= topk, cand, thr)

    thr = lax.fori_loop(0, 32, bit_body, jnp.full((1, LANES), INT_MIN, I32))
    cnt_gt = _count(keys_ref, nkb, lambda blk, _: blk > thr)
    cnt_ge = _count(keys_ref, nkb, lambda blk, _: blk >= thr)
    need = topk - cnt_gt
    tie = ((cnt_ge - cnt_gt) > need) & (thr > INT_MIN)
    return thr, need, tie


def _tie_cutoff(keys_ref, nkb, thr, need, pos_bits):
    def bit_body(j, c):
        cand = c + lax.shift_left(jnp.int32(1), pos_bits - 1 - j)
        before = _count(keys_ref, nkb, lambda blk, pos: (blk == thr) & (pos < cand))
        return jnp.where(before < need, cand, c)

    return lax.fori_loop(0, pos_bits, bit_body, jnp.zeros((1, LANES), I32))


def _threshold_and_cutoff(keys_ref, cut_ref, nkb, topk, pos_bits):
    thr, need, tie = _select_threshold(keys_ref, nkb, topk)
    cut_ref[...] = jnp.full((1, LANES), 2 ** pos_bits - 1, I32)

    @pl.when(jnp.max(tie.astype(I32)) > 0)
    def _():
        cut_ref[...] = _tie_cutoff(keys_ref, nkb, thr, need, pos_bits)

    return thr


def _selected(key, pos, thr, cut):
    return (key > thr) | ((key == thr) & (pos <= cut))


def _attn_prompt_kernel(qs_ref, qi_ref, wit_ref, kb_ref, vt_ref, kib_ref, oa_ref,
                        keys_ref, acc_ref, m_ref, l_ref, cut_ref, *, topk, pos_bits):
    i = pl.program_id(1)
    nkb = i + 1
    row_i = lax.broadcasted_iota(I32, (KT, QT), 0)
    col_i = lax.broadcasted_iota(I32, (KT, QT), 1)
    tpos = col_i + i * QT

    qi = qi_ref[...]
    qi_h = [qi[:, h * IDX_DIM:(h + 1) * IDX_DIM] for h in range(IDX_HEADS)]
    wit = wit_ref[0]

    def score_body(kb, carry):
        start = pl.multiple_of(kb * KT, KT)
        kib = kib_ref[pl.ds(start, KT), :]
        score = jnp.zeros((KT, QT), F32)
        for h in range(IDX_HEADS):
            score = score + jnp.maximum(_dot_nt(kib, qi_h[h]), 0.0) * wit[h:h + 1, :]
        key = jnp.where(row_i + start <= tpos, _score_keys(score), INT_MIN)
        keys_ref[pl.ds(start, KT), :] = key
        return carry

    lax.fori_loop(0, nkb, score_body, 0)
    thr = _threshold_and_cutoff(keys_ref, cut_ref, nkb, topk, pos_bits)
    cut = cut_ref[...]

    q = qs_ref[...]
    lane = lax.broadcasted_iota(I32, (1, LANES), 1)
    q_pad = []
    for h in range(A_HEADS):
        pair = q[:, (h // 2) * LANES:(h // 2 + 1) * LANES].astype(F32)
        q_pad.append(jnp.where((lane // HEAD_DIM) == (h % 2), pair, 0.0).astype(BF16))
    m_ref[...] = jnp.full(m_ref.shape, NEG_BIG, F32)
    l_ref[...] = jnp.zeros(l_ref.shape, F32)
    acc_ref[...] = jnp.zeros(acc_ref.shape, F32)

    def kv_body(kb, carry):
        start = pl.multiple_of(kb * KT, KT)
        spos = row_i + start
        key = keys_ref[pl.ds(start, KT), :]
        sel = _selected(key, spos, thr, cut) & (spos <= tpos)
        dist = (tpos - spos).astype(F32)
        for h in range(A_HEADS):
            kp = kb_ref[pl.ds(start, KT), (h // 2) * LANES:(h // 2 + 1) * LANES]
            s = _dot_nt(kp, q_pad[h])
            logit = jnp.where(sel, s - ALIBI_SLOPES[h] * dist, NEG_BIG)
            m_old = m_ref[h:h + 1, :]
            m_new = jnp.maximum(m_old, jnp.max(logit, axis=0, keepdims=True))
            p = jnp.exp(logit - m_new)
            alpha = jnp.exp(m_old - m_new)
            l_ref[h:h + 1, :] = alpha * l_ref[h:h + 1, :] + jnp.sum(p, axis=0, keepdims=True)
            vt = vt_ref[0, kb, h * HEAD_DIM:(h + 1) * HEAD_DIM, :]
            rows = slice(h * HEAD_DIM, (h + 1) * HEAD_DIM)
            acc_ref[rows, :] = alpha * acc_ref[rows, :] + _dot(vt, p.astype(BF16))
            m_ref[h:h + 1, :] = m_new
        return carry

    lax.fori_loop(0, nkb, kv_body, 0)
    for h in range(A_HEADS):
        rows = slice(h * HEAD_DIM, (h + 1) * HEAD_DIM)
        acc_ref[rows, :] = acc_ref[rows, :] / l_ref[h:h + 1, :]
    oa_ref[...] = acc_ref[...].T


def _attn_prompt(qs, qi, wit, kb, vt, kib, nb):
    n = qs.shape[0]
    t = n // nb
    nq = t // QT
    topk = min(TOPK_MAX, t // 4)
    pos_bits = max(1, (t - 1).bit_length())
    once = pl.Buffered(1)
    in_specs = [
        pl.BlockSpec((QT, A_WIDTH), lambda b, i: (b * nq + i, 0)),
        pl.BlockSpec((QT, IDX_HEADS * IDX_DIM), lambda b, i: (b * nq + i, 0)),
        pl.BlockSpec((1, IDX_HEADS, QT), lambda b, i: (b, 0, i)),
        pl.BlockSpec((t, A_WIDTH), lambda b, i: (b, 0), pipeline_mode=once),
        pl.BlockSpec((1, t // KT, A_WIDTH, KT), lambda b, i: (b, 0, 0, 0), pipeline_mode=once),
        pl.BlockSpec((t, IDX_DIM), lambda b, i: (b, 0), pipeline_mode=once),
    ]
    scratch = [
        pltpu.VMEM((t, QT), I32),
        pltpu.VMEM((A_WIDTH, QT), F32),
        pltpu.VMEM((A_HEADS, QT), F32),
        pltpu.VMEM((A_HEADS, QT), F32),
        pltpu.VMEM((1, QT), I32),
    ]
    kern = functools.partial(_attn_prompt_kernel, topk=topk, pos_bits=pos_bits)
    return pl.pallas_call(
        kern, grid=(nb, nq), in_specs=in_specs,
        out_specs=pl.BlockSpec((QT, A_WIDTH), lambda b, i: (b * nq + i, 0)),
        out_shape=jax.ShapeDtypeStruct((n, A_WIDTH), F32), scratch_shapes=scratch,
        compiler_params=_params(("arbitrary", "arbitrary"), 40 * 1024 * 1024), name="attn_prompt",
    )(qs, qi, wit, kb, vt, kib)


def _sample_scores_kernel(pt_ref, qi_ref, wi_ref, kin_ref, *rest):
    del pt_ref
    pages, keys_ref = rest[:-1], rest[-1]
    qi = qi_ref[...]
    wi = wi_ref[...]
    for j, page in enumerate(pages):
        sc = _dot_nt(qi, page[...].astype(BF16))
        score = jnp.sum(jnp.maximum(sc, 0.0) * wi, axis=0, keepdims=True)
        keys_ref[j:j + 1, :] = _score_keys(score)
    kin = kin_ref[...].astype(BF16).astype(F32)
    sc = jnp.sum(qi.astype(F32) * kin, axis=1, keepdims=True)
    score = jnp.sum(jnp.maximum(sc, 0.0) * wi, axis=0, keepdims=True)
    lane = lax.broadcasted_iota(I32, (1, LANES), 1)
    keys_ref[len(pages):len(pages) + 1, :] = jnp.where(lane == 0, _score_keys(score), INT_MIN)


def _sample_scores(page_table, qi, wi, ki_new, cache_idx_k, e):
    db, n_pages = page_table.shape
    in_specs = [
        pl.BlockSpec((None, IDX_HEADS, IDX_DIM), lambda b, pt: (b, 0, 0)),
        pl.BlockSpec((None, IDX_HEADS, 1), lambda b, pt: (b, 0, 0)),
        pl.BlockSpec((None, 1, IDX_DIM), lambda b, pt: (b, 0, 0)),
    ]
    for j in range(n_pages):
        in_specs.append(pl.BlockSpec((None, None, PAGE_SIZE, IDX_DIM),
                                     lambda b, pt, j=j: (e, pt[b, j], 0, 0)))
    grid_spec = pltpu.PrefetchScalarGridSpec(
        num_scalar_prefetch=1, grid=(db,), in_specs=in_specs,
        out_specs=pl.BlockSpec((None, n_pages + 1, PAGE_SIZE), lambda b, pt: (b, 0, 0)))
    return pl.pallas_call(
        _sample_scores_kernel, grid_spec=grid_spec,
        out_shape=jax.ShapeDtypeStruct((db, n_pages + 1, PAGE_SIZE), I32),
        compiler_params=_params(("arbitrary",), 16 * 1024 * 1024), name="sample_scores",
    )(page_table, qi.reshape(db, IDX_HEADS, IDX_DIM), wi.reshape(db, IDX_HEADS, 1),
      ki_new.reshape(db, 1, IDX_DIM), *([cache_idx_k] * n_pages))


def _sample_select_kernel(keys_ref, thr_ref, cut_ref, keyst_ref, *, topk, pos_bits):
    keyst_ref[...] = pltpu.bitcast(pltpu.bitcast(keys_ref[...], F32).T, I32)
    thr_ref[...] = _threshold_and_cutoff(keyst_ref, cut_ref, keyst_ref.shape[0] // KT, topk, pos_bits)


def _sample_select(keys, n_real):
    db, rows, _ = keys.shape
    assert db == LANES
    n_pos = rows * PAGE_SIZE
    kern = functools.partial(_sample_select_kernel, topk=min(TOPK_MAX, n_real // 4),
                             pos_bits=(n_pos - 1).bit_length())
    return pl.pallas_call(
        kern, out_shape=(jax.ShapeDtypeStruct((1, LANES), I32), jax.ShapeDtypeStruct((1, LANES), I32)),
        scratch_shapes=[pltpu.VMEM((n_pos, LANES), I32)],
        compiler_params=_params(None, 16 * 1024 * 1024), name="sample_select",
    )(keys.reshape(db, n_pos))


def _sample_attn_kernel(pt_ref, thr_ref, cut_ref, q_ref, kn_ref, vn_ref, keys_ref, *rest, n_pages):
    del pt_ref
    k_pages, v_pages, oa_ref = rest[:n_pages], rest[n_pages:2 * n_pages], rest[-1]
    b = pl.program_id(0)
    thr, cut = thr_ref[b], cut_ref[b]
    past = n_pages * PAGE_SIZE
    head = lax.broadcasted_iota(I32, (A_HEADS, A_WIDTH), 0)
    lane_w = lax.broadcasted_iota(I32, (A_HEADS, A_WIDTH), 1)
    own = (lane_w // HEAD_DIM) == head
    q = q_ref[...].astype(F32)
    q_bd32 = jnp.where(own, jnp.broadcast_to(q, (A_HEADS, A_WIDTH)), 0.0)
    q_bd = q_bd32.astype(BF16)
    slope = jnp.zeros((A_HEADS, 1), F32)
    hcol = lax.broadcasted_iota(I32, (A_HEADS, 1), 0)
    for h in range(A_HEADS):
        slope = jnp.where(hcol == h, ALIBI_SLOPES[h], slope)
    lane = lax.broadcasted_iota(I32, (1, PAGE_SIZE), 1)
    logits = []
    for j in range(n_pages):
        pos = lane + j * PAGE_SIZE
        s = _dot_nt(q_bd, k_pages[j][...].astype(BF16))
        sel = _selected(keys_ref[j:j + 1, :], pos, thr, cut)
        logits.append(jnp.where(sel, s - slope * (past - pos).astype(F32), NEG_BIG))
    kn = kn_ref[...].astype(BF16).astype(F32)
    s_new = jnp.sum(q_bd32 * kn, axis=1, keepdims=True)
    sel_new = _selected(keys_ref[n_pages:n_pages + 1, 0:1], past, thr, cut)
    logit_new = jnp.where(sel_new, s_new, NEG_BIG)
    m = logit_new
    for lg in logits:
        m = jnp.maximum(m, jnp.max(lg, axis=1, keepdims=True))
    p_new = jnp.exp(logit_new - m)
    l = p_new
    o = p_new * vn_ref[...].astype(BF16).astype(F32)
    for j in range(n_pages):
        p = jnp.exp(logits[j] - m)
        l = l + jnp.sum(p, axis=1, keepdims=True)
        o = o + _dot(p.astype(BF16), v_pages[j][...].astype(BF16))
    oa_ref[...] = jnp.sum(jnp.where(own, o / l, 0.0), axis=0, keepdims=True)


def _sample_attn(page_table, thr, cut, qs, k_new, v_new, keys, cache_k, cache_v, e):
    db, n_pages = page_table.shape
    ck = cache_k.reshape(cache_k.shape[0], cache_k.shape[1], PAGE_SIZE, A_WIDTH)
    cv = cache_v.reshape(cache_v.shape[0], cache_v.shape[1], PAGE_SIZE, A_WIDTH)
    per_b = lambda b, pt, th, cu: (b, 0, 0)
    in_specs = [
        pl.BlockSpec((None, 1, A_WIDTH), per_b),
        pl.BlockSpec((None, 1, A_WIDTH), per_b),
        pl.BlockSpec((None, 1, A_WIDTH), per_b),
        pl.BlockSpec((None, n_pages + 1, PAGE_SIZE), per_b),
    ]
    for _ in range(2):
        for j in range(n_pages):
            in_specs.append(pl.BlockSpec((None, None, PAGE_SIZE, A_WIDTH),
                                         lambda b, pt, th, cu, j=j: (e, pt[b, j], 0, 0)))
    grid_spec = pltpu.PrefetchScalarGridSpec(
        num_scalar_prefetch=3, grid=(db,), in_specs=in_specs,
        out_specs=pl.BlockSpec((None, 1, A_WIDTH), per_b))
    out = pl.pallas_call(
        functools.partial(_sample_attn_kernel, n_pages=n_pages), grid_spec=grid_spec,
        out_shape=jax.ShapeDtypeStruct((db, 1, A_WIDTH), F32),
        compiler_params=_params(("arbitrary",), 32 * 1024 * 1024), name="sample_attn",
    )(page_table, thr.reshape(db), cut.reshape(db), qs.reshape(db, 1, A_WIDTH),
      k_new.reshape(db, 1, A_WIDTH), v_new.reshape(db, 1, A_WIDTH), keys,
      *([ck] * n_pages), *([cv] * n_pages))
    return out.reshape(db, A_WIDTH)


def _even_post_kernel(x_ref, oa_ref, rest_ref, wout_ref, ws_ref, bias_ref, blng_ref, blnb_ref,
                      lng_ref, lnb_ref, y_ref, bv_ref, *, single_row):
    rest = rest_ref[...]
    ga, ub = rest[:, 0:A_WIDTH], rest[:, A_WIDTH:2 * A_WIDTH]
    vb, gb = rest[:, 2 * A_WIDTH:3 * A_WIDTH], rest[:, 3 * A_WIDTH:4 * A_WIDTH]
    vn = _layer_norm(vb, blng_ref[...], blnb_ref[...])
    bv_ref[...] = vn.reshape(bv_ref.shape)
    if single_row:
        mixed = vn * ws_ref[...] + bias_ref[...]
    else:
        row_i = lax.broadcasted_iota(I32, (CHUNK, CHUNK), 0)
        col_i = lax.broadcasted_iota(I32, (CHUNK, CHUNK), 1)
        lane = lax.broadcasted_iota(I32, (1, LANES), 1)
        vnb = vn.astype(BF16)
        parts = []
        for p in range(B_GROUPS // 2):
            vp = vnb[:, p * LANES:(p + 1) * LANES]
            w0 = jnp.where(row_i >= col_i, ws_ref[2 * p], 0.0).astype(BF16)
            w1 = jnp.where(row_i >= col_i, ws_ref[2 * p + 1], 0.0).astype(BF16)
            parts.append(jnp.where(lane < B_GDIM, _dot(w0, vp), _dot(w1, vp)))
        mixed = jnp.concatenate(parts, axis=1) + bias_ref[...]
    ob = ub * mixed * _silu(gb)
    oag = oa_ref[...] * _silu(ga)
    z = _dot(oag.astype(BF16), wout_ref[0:A_WIDTH, :]) + _dot(ob.astype(BF16), wout_ref[A_WIDTH:, :])
    y_ref[...] = _layer_norm(ALPHA * x_ref[...] + z, lng_ref[...], lnb_ref[...])


def _even_post(x2d, oa, rest, w_out, b_ws, b_bs, b_ln_g, b_ln_b, ln_g, ln_b, nb, single_row):
    n = x2d.shape[0]
    tm = CHUNK
    tpb = (n // nb) // tm if not single_row else 1
    row = lambda i: (i, 0)
    const2 = lambda i: (0, 0)
    if single_row:
        ws = jnp.repeat(b_ws[:, 0, 0], B_GDIM).reshape(1, B_WIDTH)
        bias = jnp.repeat(b_bs[:, 0], B_GDIM).reshape(1, B_WIDTH)
        ws_spec = pl.BlockSpec(ws.shape, const2)
        bv_shape = jax.ShapeDtypeStruct((n, B_WIDTH), F32)
        bv_spec = pl.BlockSpec((tm, B_WIDTH), row)
    else:
        ws = b_ws
        bias = jnp.repeat(b_bs.T, B_GDIM, axis=1)
        ws_spec = pl.BlockSpec(ws.shape, lambda i: (0, 0, 0))
        bv_shape = jax.ShapeDtypeStruct((nb, CHUNK, B_WIDTH), F32)
        bv_spec = pl.BlockSpec((1, CHUNK, B_WIDTH), lambda i: (i // tpb, 0, 0))
    in_specs = [
        pl.BlockSpec((tm, D_MODEL), row),
        pl.BlockSpec((tm, A_WIDTH), row),
        pl.BlockSpec((tm, 4 * A_WIDTH), row),
        pl.BlockSpec((A_WIDTH + B_WIDTH, D_MODEL), const2),
        ws_spec,
        pl.BlockSpec(bias.shape, const2),
        pl.BlockSpec((1, B_WIDTH), const2),
        pl.BlockSpec((1, B_WIDTH), const2),
        pl.BlockSpec((1, D_MODEL), const2),
        pl.BlockSpec((1, D_MODEL), const2),
    ]
    return pl.pallas_call(
        functools.partial(_even_post_kernel, single_row=single_row), grid=(n // tm,), in_specs=in_specs,
        out_specs=(pl.BlockSpec((tm, D_MODEL), row), bv_spec),
        out_shape=(jax.ShapeDtypeStruct((n, D_MODEL), F32), bv_shape),
        compiler_params=_params(("arbitrary",), 24 * 1024 * 1024), name="even_post",
    )(x2d, oa, rest, w_out.astype(BF16), ws, bias, b_ln_g.reshape(1, -1), b_ln_b.reshape(1, -1),
      ln_g.reshape(1, -1), ln_b.reshape(1, -1))


def _head_ones():
    r = lax.broadcasted_iota(I32, (LANES, LANES), 0) // C_HDIM
    c = lax.broadcasted_iota(I32, (LANES, LANES), 1) // C_HDIM
    return jnp.where(r == c, 1.0, 0.0).astype(BF16)


def _rwkv_pre_kernel(*refs, sample, has_vres, tpb):
    it = iter(refs)
    x_ref, xprev_ref, mu_ref, win_ref = next(it), next(it), next(it), next(it)
    w0_ref, w1_ref, w2_ref, a0_ref, a1_ref, a2_ref = (next(it) for _ in range(6))
    if has_vres:
        v0_ref, v1_ref, v2_ref, vfirst_ref = (next(it) for _ in range(4))
    kk_ref, ka_ref = next(it), next(it)
    r_ref, k_ref, v_ref, kkn_ref, a_ref, lw_ref, g_ref, vnat_ref = (next(it) for _ in range(8))

    x = x_ref[...]
    if sample:
        xp = xprev_ref[...]
    else:
        i = pl.program_id(0)
        prev_row = jnp.where(i % tpb == 0, 0.0, xprev_ref[SUBLANES - 1:SUBLANES, :])
        row_i = lax.broadcasted_iota(I32, x.shape, 0)
        xp = jnp.where(row_i == 0, prev_row, pltpu.roll(x, 1, axis=0))
    xx = xp - x
    mix = lambda j: (x + xx * mu_ref[j:j + 1, :])
    xr, xw, xk, xv, xa, xg = (mix(j) for j in range(6))
    r = _dot(xr.astype(BF16), win_ref[0])
    k = _dot(xk.astype(BF16), win_ref[1])
    v = _dot(xv.astype(BF16), win_ref[2])
    g = _dot(xg.astype(BF16), win_ref[3])
    z = w0_ref[...] + _dot(jnp.tanh(_dot(xw.astype(BF16), w1_ref[...])).astype(BF16), w2_ref[...])
    w_log = -(jnp.maximum(-z, 0.0) + jnp.log1p(jnp.exp(-jnp.abs(z)))) - 0.5
    lw = -jnp.exp(w_log)
    a = _sigmoid(a0_ref[...] + _dot(_dot(xa.astype(BF16), a1_ref[...]).astype(BF16), a2_ref[...]))
    if has_vres:
        gate = _sigmoid(v0_ref[...] + _dot(_dot(xv.astype(BF16), v1_ref[...]).astype(BF16), v2_ref[...]))
        v = v + (vfirst_ref[...] - v) * gate
    kk = k * kk_ref[...]
    ones_bd = _head_ones()
    ss = jnp.concatenate(
        [_dot_exact_rhs(jnp.square(kk[:, p * LANES:(p + 1) * LANES]), ones_bd) for p in range(N_PAIRS)], axis=1)
    kkn = kk / jnp.maximum(jnp.sqrt(ss), 1e-12)
    k2 = k * (1.0 + (a - 1.0) * ka_ref[...])
    g_ref[...] = g
    vnat_ref[...] = v
    outs = ((r_ref, r), (k_ref, k2), (v_ref, v), (kkn_ref, kkn), (a_ref, a), (lw_ref, lw))
    if sample:
        for ref, val in outs:
            ref[...] = val.T
    else:
        for ref, val in outs:
            for p in range(N_PAIRS):
                ref[0, p] = val[:, p * LANES:(p + 1) * LANES]


def _rwkv_pre(x2d, nb, sample, shift_prev, w, v_first):
    n = x2d.shape[0]
    t = n // nb
    tm = LANES if sample else min(128, t)
    tpb = max(t // tm, 1)
    has_vres = v_first is not None
    row = lambda i: (i, 0)
    const2 = lambda i: (0, 0)
    vec = lambda a: a.reshape(1, -1)
    args = [x2d]
    in_specs = [pl.BlockSpec((tm, D_MODEL), row)]
    if sample:
        args.append(shift_prev)
        in_specs.append(pl.BlockSpec((tm, D_MODEL), row))
    else:
        args.append(x2d)
        in_specs.append(pl.BlockSpec((SUBLANES, D_MODEL),
                                     lambda i: (jnp.maximum(i * (tm // SUBLANES) - 1, 0), 0)))
    args += [w["mu"], w["w_in"].astype(BF16), vec(w["w0"]), w["w1"].astype(BF16), w["w2"].astype(BF16),
             vec(w["a0"]), w["a1"].astype(BF16), w["a2"].astype(BF16)]
    in_specs += [pl.BlockSpec((6, D_MODEL), const2), pl.BlockSpec((4, D_MODEL, D_MODEL), lambda i: (0, 0, 0))]
    in_specs += [pl.BlockSpec(a.shape, const2) for a in args[4:]]
    if has_vres:
        extra = [vec(w["v0"]), w["v1"].astype(BF16), w["v2"].astype(BF16)]
        args += extra + [v_first]
        in_specs += [pl.BlockSpec(a.shape, const2) for a in extra] + [pl.BlockSpec((tm, D_MODEL), row)]
    args += [vec(w["kk"]), vec(w["ka"])]
    in_specs += [pl.BlockSpec((1, D_MODEL), const2)] * 2
    if sample:
        scan_shape = jax.ShapeDtypeStruct((D_MODEL, n), F32)
        scan_spec = pl.BlockSpec((D_MODEL, tm), lambda i: (0, i))
    else:
        scan_shape = jax.ShapeDtypeStruct((nb, N_PAIRS, t, LANES), F32)
        scan_spec = pl.BlockSpec((1, N_PAIRS, tm, LANES), lambda i: (i // tpb, 0, i % tpb, 0))
    nat_shape = jax.ShapeDtypeStruct((n, D_MODEL), F32)
    nat_spec = pl.BlockSpec((tm, D_MODEL), row)
    outs = pl.pallas_call(
        functools.partial(_rwkv_pre_kernel, sample=sample, has_vres=has_vres, tpb=tpb),
        grid=(n // tm,), in_specs=in_specs,
        out_specs=(scan_spec,) * 6 + (nat_spec,) * 2, out_shape=(scan_shape,) * 6 + (nat_shape,) * 2,
        compiler_params=_params(("arbitrary",), 56 * 1024 * 1024), name="rwkv_pre",
    )(*args)
    return outs[:6], outs[6], outs[7]


def _rwkv_scan_kernel(r_ref, k_ref, v_ref, kk_ref, a_ref, lw_ref, o_ref, sfin_ref, m_ref):
    c = pl.program_id(1)
    cs = RW_CHUNK
    rows = 2 * cs

    @pl.when(c == 0)
    def _():
        m_ref[...] = jnp.zeros(m_ref.shape, F32)

    ri = lax.broadcasted_iota(I32, (rows, rows), 0)
    ci = lax.broadcasted_iota(I32, (rows, rows), 1)
    same = (ri // cs) == (ci // cs)
    tr, tc = ri % cs, ci % cs
    strict = same & (tr > tc)
    incl = same & (tr >= tc)
    eye = jnp.where(ri == ci, 1.0, 0.0).astype(F32)
    tri = jnp.where(lax.broadcasted_iota(I32, (cs, cs), 0) >= lax.broadcasted_iota(I32, (cs, cs), 1),
                    1.0, 0.0).astype(F32)
    head0 = lax.broadcasted_iota(I32, (1, LANES), 1) < C_HDIM

    def stack(y):
        return jnp.concatenate([jnp.where(head0, y, 0.0), jnp.where(head0, 0.0, y)], axis=0)

    def one_pair(p):
        r, k, v = r_ref[0, p], k_ref[0, p], v_ref[0, p]
        kk, a, lw = kk_ref[0, p], a_ref[0, p], lw_ref[0, p]
        cum = _hdot(tri, lw)
        tot = cum[cs - 1:cs, :]
        b = kk * a
        e_neg = jnp.exp(-cum)
        e_tail = jnp.exp(tot - cum)
        kap_s = stack(kk * jnp.exp(cum - lw))
        bt_s, kt_s = stack(b * e_neg), stack(k * e_neg)
        rt_s = stack(r * jnp.exp(cum))
        bh_s, kh_s = stack(b * e_tail), stack(k * e_tail)
        v_s = stack(v)
        gram = _hdot_nt(jnp.concatenate([kap_s, rt_s], axis=0), jnp.concatenate([bt_s, kt_s], axis=0))
        l_mat = jnp.where(strict, gram[:rows, :rows], 0.0)
        a_kk = jnp.where(strict, gram[:rows, rows:], 0.0)
        a_rb = jnp.where(incl, gram[rows:, :rows], 0.0)
        a_rk = jnp.where(incl, gram[rows:, rows:], 0.0)
        tinv = eye
        s = 1
        while s < cs:
            off = same & ((tr & s) != 0) & ((tc & s) == 0) & ((tr // (2 * s)) == (tc // (2 * s)))
            tinv = tinv - _hdot(tinv, _hdot(jnp.where(off, l_mat, 0.0), tinv))
            s *= 2
        kp = _hdot(tinv, kap_s)
        u0 = -_hdot(tinv, _hdot(a_kk, v_s))
        rp = rt_s - _hdot(a_rb, kp)
        o0 = _hdot(a_rb, u0) + _hdot(a_rk, v_s)
        trans = jnp.where(lax.broadcasted_iota(I32, (LANES, LANES), 0) == lax.broadcasted_iota(I32, (LANES, LANES), 1),
                          jnp.exp(tot), 0.0) - _hdot_tn(bh_s, kp)
        gain = _hdot_tn(bh_s, u0) + _hdot_tn(kh_s, v_s)
        m0 = m_ref[p]
        o = _hdot(rp, m0) + o0
        o_ref[0, p] = o[:cs] + o[cs:]
        m_ref[p] = _hdot(trans, m0) + gain

    def pair_body(j, carry):
        one_pair(2 * j)
        one_pair(2 * j + 1)
        return carry

    lax.fori_loop(0, N_PAIRS // 2, pair_body, 0)

    @pl.when(c == pl.num_programs(1) - 1)
    def _():
        sfin_ref[0] = m_ref[...]


def _rwkv_scan(scan_in, nb):
    t = scan_in[0].shape[2]
    nc = t // RW_CHUNK
    blk = pl.BlockSpec((1, N_PAIRS, RW_CHUNK, LANES), lambda b, c: (b, 0, c, 0))
    return pl.pallas_call(
        _rwkv_scan_kernel, grid=(nb, nc), in_specs=[blk] * 6,
        out_specs=(blk, pl.BlockSpec((1, N_PAIRS, LANES, LANES), lambda b, c: (b, 0, 0, 0))),
        out_shape=(jax.ShapeDtypeStruct((nb, N_PAIRS, t, LANES), F32),
                   jax.ShapeDtypeStruct((nb, N_PAIRS, LANES, LANES), F32)),
        scratch_shapes=[pltpu.VMEM((N_PAIRS, LANES, LANES), F32)],
        compiler_params=_params(("arbitrary", "arbitrary"), 32 * 1024 * 1024), name="rwkv_scan",
    )(*scan_in)


def _pair_state_to_heads(m):
    nb = m.shape[0]
    m = m.reshape(nb, N_PAIRS, 2, C_HDIM, 2, C_HDIM)
    heads = jnp.stack([m[:, :, 0, :, 0, :], m[:, :, 1, :, 1, :]], axis=2)
    return jnp.swapaxes(heads.reshape(nb, C_HEADS, C_HDIM, C_HDIM), -1, -2)


def _rwkv_step_kernel(s_ref, r_ref, k_ref, v_ref, kk_ref, a_ref, lw_ref, snew_ref, o_ref):
    r, k, v = r_ref[...], k_ref[...], v_ref[...]
    kk, a = kk_ref[...], a_ref[...]
    w = jnp.exp(lw_ref[...])
    nkk = -kk
    b = kk * a

    def body(vi, carry):
        s_v = s_ref[0, vi]
        sa = jnp.sum(s_v * nkk, axis=0, keepdims=True)
        s_n = s_v * w + sa * b + v_ref[pl.ds(vi, 1), :] * k
        snew_ref[0, vi] = s_n
        o_ref[pl.ds(vi, 1), :] = jnp.sum(s_n * r, axis=0, keepdims=True)
        return carry

    lax.fori_loop(0, C_HDIM, body, 0)


def _rwkv_step(state_t, scan_in):
    db = state_t.shape[-1]
    vec = pl.BlockSpec((C_HDIM, db), lambda h: (h, 0))
    st = pl.BlockSpec((1, C_HDIM, C_HDIM, db), lambda h: (h, 0, 0, 0))
    return pl.pallas_call(
        _rwkv_step_kernel, grid=(C_HEADS,), in_specs=[st] + [vec] * 6, out_specs=(st, vec),
        out_shape=(jax.ShapeDtypeStruct(state_t.shape, F32), jax.ShapeDtypeStruct((D_MODEL, db), F32)),
        compiler_params=_params(("arbitrary",), 24 * 1024 * 1024), name="rwkv_step",
    )(state_t, *scan_in)


def _rwkv_post_kernel(x_ref, o_ref, r_ref, k_ref, v_ref, g_ref, rk_ref, lnxg_ref, lnxb_ref, wout_ref,
                      lng_ref, lnb_ref, y_ref, *, sample):
    if sample:
        o_all, r_all, k_all, v_all = o_ref[...].T, r_ref[...].T, k_ref[...].T, v_ref[...].T
        piece = lambda full, p: full[:, p * LANES:(p + 1) * LANES]
        get = lambda p: (piece(o_all, p), piece(r_all, p), piece(k_all, p), piece(v_all, p))
    else:
        get = lambda p: (o_ref[0, p], r_ref[0, p], k_ref[0, p], v_ref[0, p])
    ones_bd = _head_ones()
    parts = []
    for p in range(N_PAIRS):
        o, r, k, v = get(p)
        lanes = slice(p * LANES, (p + 1) * LANES)
        mean = _dot_exact_rhs(o, ones_bd) * (1.0 / C_HDIM)
        d = o - mean
        var = _dot_exact_rhs(d * d, ones_bd) * (1.0 / C_HDIM)
        on = d * lax.rsqrt(var + GN_EPS) * lnxg_ref[:, lanes] + lnxb_ref[:, lanes]
        bonus = _dot_exact_rhs(r * k * rk_ref[:, lanes], ones_bd) * v
        parts.append(on + bonus)
    out = jnp.concatenate(parts, axis=1) * _silu(g_ref[...])
    z = _dot(out.astype(BF16), wout_ref[...])
    y_ref[...] = _layer_norm(ALPHA * x_ref[...] + z, lng_ref[...], lnb_ref[...])


def _rwkv_post(x2d, o, scan_in, g, w, ln_g, ln_b, nb, sample):
    n = x2d.shape[0]
    t = n // nb
    tm = LANES if sample else min(256, t)
    tpb = max(t // tm, 1)
    row = lambda i: (i, 0)
    const2 = lambda i: (0, 0)
    vec = lambda a: a.reshape(1, -1)
    if sample:
        scan_spec = pl.BlockSpec((D_MODEL, tm), lambda i: (0, i))
    else:
        scan_spec = pl.BlockSpec((1, N_PAIRS, tm, LANES), lambda i: (i // tpb, 0, i % tpb, 0))
    r, k, v = scan_in[0], scan_in[1], scan_in[2]
    in_specs = [pl.BlockSpec((tm, D_MODEL), row), scan_spec, scan_spec, scan_spec, scan_spec,
                pl.BlockSpec((tm, D_MODEL), row)]
    in_specs += [pl.BlockSpec((1, D_MODEL), const2)] * 3 + [pl.BlockSpec((D_MODEL, D_MODEL), const2)]
    in_specs += [pl.BlockSpec((1, D_MODEL), const2)] * 2
    return pl.pallas_call(
        functools.partial(_rwkv_post_kernel, sample=sample), grid=(n // tm,), in_specs=in_specs,
        out_specs=pl.BlockSpec((tm, D_MODEL), row), out_shape=jax.ShapeDtypeStruct((n, D_MODEL), F32),
        compiler_params=_params(("arbitrary",), 32 * 1024 * 1024), name="rwkv_post",
    )(x2d, o, r, k, v, g, vec(w["rk"]), vec(w["lnx_g"]), vec(w["lnx_b"]), w["w_out"].astype(BF16),
      vec(ln_g), vec(ln_b))


def kernel(x_prompt, x_sample, cache_k, cache_v, cache_idx_k, state_wkv, state_shift, page_table, ln_g, ln_b, e_w_in, e_w_out, b_ws, b_bs, b_ln_g, b_ln_b, c_mu, c_w_in, c_w0, c_w1, c_w2, c_a0, c_a1, c_a2, c_v0, c_v1, c_v2, c_kk, c_ka, c_rk, c_lnx_g, c_lnx_b, c_w_out):
    bp, seq, _ = x_prompt.shape
    db, dec_seq, _ = x_sample.shape
    assert dec_seq == 1 and db == LANES and seq % 256 == 0
    n_pages = page_table.shape[1]
    past = n_pages * PAGE_SIZE

    def odd_weights(o):
        w = dict(mu=c_mu[o], w_in=c_w_in[o], w0=c_w0[o], w1=c_w1[o], w2=c_w2[o], a0=c_a0[o], a1=c_a1[o],
                 a2=c_a2[o], kk=c_kk[o], ka=c_ka[o], rk=c_rk[o], lnx_g=c_lnx_g[o], lnx_b=c_lnx_b[o],
                 w_out=c_w_out[o])
        if o > 0:
            w.update(v0=c_v0[o - 1], v1=c_v1[o - 1], v2=c_v2[o - 1])
        return w

    x = x_prompt.reshape(bp * seq, D_MODEL)
    p_k, p_v, p_ik, p_bv, p_wkv, p_shift = [], [], [], [], [], []
    v_first = None
    for l in range(DEPTH):
        if l % 2 == 0:
            e = l // 2
            qs, k, kb, v, vt, qi, ki, kib, wit, rest = _even_proj(x, e_w_in[e], bp)
            oa = _attn_prompt(qs, qi, wit, kb, vt, kib, bp)
            x, bv = _even_post(x, oa, rest, e_w_out[e], b_ws[e], b_bs[e], b_ln_g[e], b_ln_b[e],
                               ln_g[l], ln_b[l], bp, single_row=False)
            p_k.append(k.reshape(bp, seq, A_HEADS, HEAD_DIM))
            p_v.append(v.reshape(bp, seq, A_HEADS, HEAD_DIM))
            p_ik.append(ki.reshape(bp, seq, IDX_DIM))
            p_bv.append(bv)
        else:
            o = l // 2
            w = odd_weights(o)
            p_shift.append(x.reshape(bp, seq, D_MODEL)[:, -1])
            scan_in, g, v_nat = _rwkv_pre(x, bp, False, None, w, v_first)
            if o == 0:
                v_first = v_nat
            out, m_fin = _rwkv_scan(scan_in, bp)
            p_wkv.append(_pair_state_to_heads(m_fin))
            x = _rwkv_post(x, out, scan_in, g, w, ln_g[l], ln_b[l], bp, sample=False)
    y_prompt = x.reshape(bp, seq, D_MODEL)

    x = x_sample.reshape(db, D_MODEL)
    s_k, s_v, s_ik, s_bv, s_wkv, s_shift = [], [], [], [], [], []
    v_first = None
    for l in range(DEPTH):
        if l % 2 == 0:
            e = l // 2
            qs, k, kb, v, vt, qi, ki, kib, wit, rest = _even_proj(x, e_w_in[e], 1)
            keys = _sample_scores(page_table, qi, wit[0].T, ki, cache_idx_k, e)
            thr, cut = _sample_select(keys, past + 1)
            oa = _sample_attn(page_table, thr, cut, qs, k, v, keys, cache_k, cache_v, e)
            x, bv = _even_post(x, oa, rest, e_w_out[e], b_ws[e], b_bs[e], b_ln_g[e], b_ln_b[e],
                               ln_g[l], ln_b[l], 1, single_row=True)
            s_k.append(k.reshape(db, 1, A_HEADS, HEAD_DIM))
            s_v.append(v.reshape(db, 1, A_HEADS, HEAD_DIM))
            s_ik.append(ki.reshape(db, 1, IDX_DIM))
            s_bv.append(bv.reshape(db, 1, B_WIDTH))
        else:
            o = l // 2
            w = odd_weights(o)
            s_shift.append(x)
            scan_in, g, v_nat = _rwkv_pre(x, 1, True, state_shift[o], w, v_first)
            if o == 0:
                v_first = v_nat
            state_t = jnp.transpose(state_wkv[o], (1, 2, 3, 0))
            new_t, out_t = _rwkv_step(state_t, scan_in)
            s_wkv.append(jnp.transpose(new_t, (3, 0, 1, 2)))
            x = _rwkv_post(x, out_t, scan_in, g, w, ln_g[l], ln_b[l], 1, sample=True)
    y_sample = x.reshape(db, 1, D_MODEL)

    return (y_prompt, y_sample, jnp.stack(p_k), jnp.stack(p_v), jnp.stack(p_ik), jnp.stack(p_bv),
            jnp.stack(p_wkv), jnp.stack(p_shift), jnp.stack(s_k), jnp.stack(s_v), jnp.stack(s_ik),
            jnp.stack(s_bv), jnp.stack(s_wkv), jnp.stack(s_shift))
```

```python
import functools

import jax
import jax.numpy as jnp
from jax import lax
from jax.experimental import pallas as pl
from jax.experimental.pallas import tpu as pltpu

F32, BF16, I32 = jnp.float32, jnp.bfloat16, jnp.int32

D_MODEL = 1024
DEPTH = 4
HEAD_DIM = 64
A_HEADS = 8
A_WIDTH = A_HEADS * HEAD_DIM
IDX_HEADS = 8
IDX_DIM = 64
TOPK_MAX = 256
B_GROUPS = 8
B_GDIM = 64
B_WIDTH = B_GROUPS * B_GDIM
CHUNK = 128
C_HEADS = 16
C_HDIM = 64
PAGE_SIZE = 128
ALPHA = (2 * DEPTH) ** 0.25
LN_EPS = 1e-5
GN_EPS = 64e-5
_OFF_QKVQI = 0
_OFF_KI = 2048
_OFF_WI = 2112
_OFF_REST = 2120
ALIBI_SLOPES = tuple(2.0 ** (-8.0 * (h + 1.0) / A_HEADS) for h in range(A_HEADS))

LANES = 128
SUBLANES = 8
V7X_VMEM_BYTES = 64 * 1024 * 1024

QT = 128
KT = 512
VT_BLOCK = 256
RW_CHUNK = 64
PAIR = 2 * C_HDIM
N_PAIRS = C_HEADS // 2
INT_MIN = -2 ** 31
NEG_BIG = -1e30


def _dot(a, b):
    return jnp.dot(a, b, preferred_element_type=F32)


def _dot_nt(a, b):
    return lax.dot_general(a, b, (((1,), (1,)), ((), ())), preferred_element_type=F32)


def _split2(x):
    hi = x.astype(BF16)
    return hi, (x - hi.astype(F32)).astype(BF16)


def _split3(x):
    hi = x.astype(BF16)
    r1 = x - hi.astype(F32)
    mid = r1.astype(BF16)
    return hi, mid, (r1 - mid.astype(F32)).astype(BF16)


def _mm3(a2, b2, dot=_dot):
    return dot(a2[0], b2[0]) + dot(a2[1], b2[0]) + dot(a2[0], b2[1])


def _dot_exact_rhs(x, a_bf16):
    hi, mid, lo = _split3(x)
    return _dot(hi, a_bf16) + _dot(mid, a_bf16) + _dot(lo, a_bf16)


def _dot_exact_lhs(a_bf16, x):
    hi, mid, lo = _split3(x)
    return _dot(a_bf16, hi) + _dot(a_bf16, mid) + _dot(a_bf16, lo)


def _layer_norm(x, g, b, eps=LN_EPS):
    mu = jnp.mean(x, axis=-1, keepdims=True)
    d = x - mu
    var = jnp.mean(d * d, axis=-1, keepdims=True)
    return d * lax.rsqrt(var + eps) * g + b


def _sigmoid(x):
    return 1.0 / (1.0 + jnp.exp(-x))


def _silu(x):
    return x * _sigmoid(x)


def _params(semantics, vmem_bytes):
    assert vmem_bytes <= V7X_VMEM_BYTES
    return pltpu.CompilerParams(dimension_semantics=semantics, vmem_limit_bytes=vmem_bytes)


def _even_proj_kernel(x_ref, wa_ref, wki_ref, wwit_ref, wvt_ref, wr_ref,
                      qs_ref, k_ref, kb_ref, v_ref, vt_ref, qi_ref, ki_ref, kib_ref, wit_ref, rest_ref):
    x = x_ref[...].astype(BF16)
    h = _dot(x, wa_ref[...])
    qs_ref[...] = (h[:, 0:A_WIDTH] * (HEAD_DIM ** -0.5)).astype(BF16)
    k = h[:, A_WIDTH:2 * A_WIDTH]
    k_ref[...] = k
    kb_ref[...] = k.astype(BF16)
    v_ref[...] = h[:, 2 * A_WIDTH:3 * A_WIDTH]
    qi_ref[...] = (h[:, 3 * A_WIDTH:4 * A_WIDTH] * (IDX_DIM ** -0.5)).astype(BF16)
    ki = _dot(x, wki_ref[...])
    ki_ref[...] = ki
    kib_ref[...] = ki.astype(BF16)
    wit_ref[0] = _dot_nt(wwit_ref[...], x) * (IDX_HEADS ** -0.5)
    vt = _dot_nt(wvt_ref[...], x)
    kt = vt_ref.shape[3]
    for j in range(vt_ref.shape[1]):
        vt_ref[0, j] = vt[:, j * kt:(j + 1) * kt].astype(BF16)
    rest_ref[...] = _dot(x, wr_ref[...])


def _even_proj(x2d, w_in, nb):
    n = x2d.shape[0]
    t = n // nb
    tm = min(256, t)
    tpb = t // tm
    kt = min(VT_BLOCK, t)
    wa = w_in[:, _OFF_QKVQI:_OFF_KI].astype(BF16)
    wki = w_in[:, _OFF_KI:_OFF_WI].astype(BF16)
    wwit = w_in[:, _OFF_WI:_OFF_REST].T.astype(BF16)
    wvt = w_in[:, 2 * A_WIDTH:3 * A_WIDTH].T.astype(BF16)
    wr = w_in[:, _OFF_REST:].astype(BF16)
    row = lambda i: (i, 0)
    const = lambda i: (0, 0)
    out_shape = (
        jax.ShapeDtypeStruct((n, A_WIDTH), BF16),
        jax.ShapeDtypeStruct((n, A_WIDTH), F32),
        jax.ShapeDtypeStruct((n, A_WIDTH), BF16),
        jax.ShapeDtypeStruct((n, A_WIDTH), F32),
        jax.ShapeDtypeStruct((nb, t // kt, A_WIDTH, kt), BF16),
        jax.ShapeDtypeStruct((n, IDX_HEADS * IDX_DIM), BF16),
        jax.ShapeDtypeStruct((n, IDX_DIM), F32),
        jax.ShapeDtypeStruct((n, IDX_DIM), BF16),
        jax.ShapeDtypeStruct((nb, IDX_HEADS, t), F32),
        jax.ShapeDtypeStruct((n, 4 * A_WIDTH), F32),
    )
    out_specs = (
        pl.BlockSpec((tm, A_WIDTH), row),
        pl.BlockSpec((tm, A_WIDTH), row),
        pl.BlockSpec((tm, A_WIDTH), row),
        pl.BlockSpec((tm, A_WIDTH), row),
        pl.BlockSpec((1, tm // kt, A_WIDTH, kt), lambda i: (i // tpb, i % tpb, 0, 0)),
        pl.BlockSpec((tm, IDX_HEADS * IDX_DIM), row),
        pl.BlockSpec((tm, IDX_DIM), row),
        pl.BlockSpec((tm, IDX_DIM), row),
        pl.BlockSpec((1, IDX_HEADS, tm), lambda i: (i // tpb, 0, i % tpb)),
        pl.BlockSpec((tm, 4 * A_WIDTH), row),
    )
    in_specs = [
        pl.BlockSpec((tm, D_MODEL), row),
        pl.BlockSpec(wa.shape, const),
        pl.BlockSpec(wki.shape, const),
        pl.BlockSpec(wwit.shape, const),
        pl.BlockSpec(wvt.shape, const),
        pl.BlockSpec(wr.shape, const),
    ]
    return pl.pallas_call(
        _even_proj_kernel, grid=(n // tm,), in_specs=in_specs, out_specs=out_specs, out_shape=out_shape,
        compiler_params=_params(("arbitrary",), 48 * 1024 * 1024), name="even_proj",
    )(x2d, wa, wki, wwit, wvt, wr)


def _score_keys(score):
    bits = pltpu.bitcast(score + 0.0, I32)
    return jnp.where(bits < 0, bits ^ jnp.int32(0x7FFFFFFF), bits)


def _count(keys_ref, nkb, kt, pred):
    row_i = lax.broadcasted_iota(I32, (kt, LANES), 0)

    def body(kb, acc):
        start = pl.multiple_of(kb * kt, kt)
        blk = keys_ref[pl.ds(start, kt), :]
        hit = jnp.where(pred(blk, row_i + start), 1, 0).astype(I32)
        return acc + jnp.sum(hit.reshape(kt // SUBLANES, SUBLANES, LANES), axis=0)

    acc = lax.fori_loop(0, nkb, body, jnp.zeros((SUBLANES, LANES), I32))
    return jnp.sum(acc, axis=0, keepdims=True)


def _select_threshold(keys_ref, nkb, kt, topk):
    def bit_body(j, thr):
        cand = thr + lax.shift_left(jnp.int32(1), 31 - j)
        cnt = _count(keys_ref, nkb, kt, lambda blk, _: blk >= cand)
        return jnp.where(cnt >= topk, cand, thr)

    thr = lax.fori_loop(0, 32, bit_body, jnp.full((1, LANES), INT_MIN, I32))
    cnt_gt = _count(keys_ref, nkb, kt, lambda blk, _: blk > thr)
    cnt_ge = _count(keys_ref, nkb, kt, lambda blk, _: blk >= thr)
    need = topk - cnt_gt
    tie = ((cnt_ge - cnt_gt) > need) & (thr > INT_MIN)
    return thr, need, tie


def _tie_cutoff(keys_ref, nkb, kt, thr, need, pos_bits):
    def bit_body(j, c):
        cand = c + lax.shift_left(jnp.int32(1), pos_bits - 1 - j)
        before = _count(keys_ref, nkb, kt, lambda blk, pos: (blk == thr) & (pos < cand))
        return jnp.where(before < need, cand, c)

    return lax.fori_loop(0, pos_bits, bit_body, jnp.zeros((1, LANES), I32))


def _threshold_and_cutoff(keys_ref, cut_ref, nkb, kt, topk, pos_bits):
    thr, need, tie = _select_threshold(keys_ref, nkb, kt, topk)
    cut_ref[...] = jnp.full((1, LANES), 2 ** pos_bits - 1, I32)

    @pl.when(jnp.max(tie.astype(I32)) > 0)
    def _():
        cut_ref[...] = _tie_cutoff(keys_ref, nkb, kt, thr, need, pos_bits)

    return thr


def _selected(key, pos, thr, cut):
    return (key > thr) | ((key == thr) & (pos <= cut))


def _attn_prompt_kernel(qs_ref, qi_ref, wit_ref, kb_ref, vt_ref, kib_ref, oa_ref,
                        keys_ref, acc_ref, cut_ref, s_ref, p_ref, *, topk, pos_bits):
    i = pl.program_id(1)
    nkb = (i * QT + QT + KT - 1) // KT
    row_i = lax.broadcasted_iota(I32, (KT, QT), 0)
    col_i = lax.broadcasted_iota(I32, (KT, QT), 1)
    tpos = col_i + i * QT

    qi = qi_ref[...]
    qi_h = [qi[:, h * IDX_DIM:(h + 1) * IDX_DIM] for h in range(IDX_HEADS)]
    qi_two = [jnp.concatenate([qi_h[2 * j], qi_h[2 * j + 1]], axis=0) for j in range(IDX_HEADS // 2)]
    wit = wit_ref[0]

    def score_body(kb, carry):
        start = pl.multiple_of(kb * KT, KT)
        kib = kib_ref[pl.ds(start, KT), :]
        for j in range(IDX_HEADS // 2):
            two = _dot_nt(kib, qi_two[j])
            s_ref[2 * j] = two[:, :QT]
            s_ref[2 * j + 1] = two[:, QT:]
        score = jnp.zeros((KT, QT), F32)
        for h in range(IDX_HEADS):
            score = score + jnp.maximum(s_ref[h], 0.0) * wit[h:h + 1, :]
        key = jnp.where(row_i + start <= tpos, _score_keys(score), INT_MIN)
        keys_ref[pl.ds(start, KT), :] = key
        return carry

    lax.fori_loop(0, nkb, score_body, 0)
    thr = _threshold_and_cutoff(keys_ref, cut_ref, nkb, KT, topk, pos_bits)
    cut = cut_ref[...]

    q = qs_ref[...]
    lane = lax.broadcasted_iota(I32, (1, LANES), 1)
    q_pad = []
    for h in range(A_HEADS):
        pair = q[:, (h // 2) * LANES:(h // 2 + 1) * LANES].astype(F32)
        q_pad.append(jnp.where((lane // HEAD_DIM) == (h % 2), pair, 0.0).astype(BF16))
    q_two = [jnp.concatenate([q_pad[2 * j], q_pad[2 * j + 1]], axis=0) for j in range(A_HEADS // 2)]
    acc_ref[...] = jnp.zeros(acc_ref.shape, F32)

    def kv_body(kb, carry):
        m_all, l_all = carry
        start = pl.multiple_of(kb * KT, KT)
        spos = row_i + start
        key = keys_ref[pl.ds(start, KT), :]
        sel = _selected(key, spos, thr, cut) & (spos <= tpos)
        dist = (tpos - spos).astype(F32)
        m_out, l_out, alphas = [], [], []
        for j in range(A_HEADS // 2):
            kp = kb_ref[pl.ds(start, KT), j * LANES:(j + 1) * LANES]
            two = _dot_nt(kp, q_two[j])
            s_ref[2 * j] = two[:, :QT]
            s_ref[2 * j + 1] = two[:, QT:]
        for h in range(A_HEADS):
            logit = jnp.where(sel, s_ref[h] - ALIBI_SLOPES[h] * dist, NEG_BIG)
            m_new = jnp.maximum(m_all[h], jnp.max(logit, axis=0, keepdims=True))
            p = jnp.exp(logit - m_new)
            alpha = jnp.exp(m_all[h] - m_new)
            l_out.append(alpha * l_all[h] + jnp.sum(p, axis=0, keepdims=True))
            m_out.append(m_new)
            alphas.append(alpha)
            p_ref[h] = p.astype(BF16)
        for h in range(A_HEADS):
            hrows = slice(h * HEAD_DIM, (h + 1) * HEAD_DIM)
            per_trip = KT // VT_BLOCK
            vt = jnp.concatenate([vt_ref[0, kb * per_trip + j, hrows, :] for j in range(per_trip)], axis=1)
            rows = slice(h * HEAD_DIM, (h + 1) * HEAD_DIM)
            acc_ref[rows, :] = alphas[h] * acc_ref[rows, :] + _dot(vt, p_ref[h])
        return tuple(m_out), tuple(l_out)

    init = (tuple(jnp.full((1, QT), NEG_BIG, F32) for _ in range(A_HEADS)),
            tuple(jnp.zeros((1, QT), F32) for _ in range(A_HEADS)))
    _, l_fin = lax.fori_loop(0, nkb, kv_body, init)
    for h in range(A_HEADS):
        rows = slice(h * HEAD_DIM, (h + 1) * HEAD_DIM)
        acc_ref[rows, :] = acc_ref[rows, :] / l_fin[h]
    oa_ref[...] = acc_ref[...].T


def _attn_prompt(qs, qi, wit, kb, vt, kib, nb):
    n = qs.shape[0]
    t = n // nb
    nq = t // QT
    topk = min(TOPK_MAX, t // 4)
    pos_bits = max(1, (t - 1).bit_length())
    once = pl.Buffered(1)
    in_specs = [
        pl.BlockSpec((QT, A_WIDTH), lambda b, i: (b * nq + i, 0)),
        pl.BlockSpec((QT, IDX_HEADS * IDX_DIM), lambda b, i: (b * nq + i, 0)),
        pl.BlockSpec((1, IDX_HEADS, QT), lambda b, i: (b, 0, i)),
        pl.BlockSpec((t, A_WIDTH), lambda b, i: (b, 0), pipeline_mode=once),
        pl.BlockSpec((1, t // VT_BLOCK, A_WIDTH, VT_BLOCK), lambda b, i: (b, 0, 0, 0), pipeline_mode=once),
        pl.BlockSpec((t, IDX_DIM), lambda b, i: (b, 0), pipeline_mode=once),
    ]
    scratch = [
        pltpu.VMEM((t, QT), I32),
        pltpu.VMEM((A_WIDTH, QT), F32),
        pltpu.VMEM((1, QT), I32),
        pltpu.VMEM((A_HEADS, KT, QT), F32),
        pltpu.VMEM((A_HEADS, KT, QT), BF16),
    ]
    kern = functools.partial(_attn_prompt_kernel, topk=topk, pos_bits=pos_bits)
    return pl.pallas_call(
        kern, grid=(nb, nq), in_specs=in_specs,
        out_specs=pl.BlockSpec((QT, A_WIDTH), lambda b, i: (b * nq + i, 0)),
        out_shape=jax.ShapeDtypeStruct((n, A_WIDTH), F32), scratch_shapes=scratch,
        compiler_params=_params(("arbitrary", "arbitrary"), 40 * 1024 * 1024), name="attn_prompt",
    )(qs, qi, wit, kb, vt, kib)


def _sample_scores_kernel(pt_ref, qi_ref, wi_ref, kin_ref, *rest):
    del pt_ref
    pages, keys_ref = rest[:-1], rest[-1]
    qi = qi_ref[...]
    wi = wi_ref[...]
    for j, page in enumerate(pages):
        sc = _dot_nt(qi, page[...].astype(BF16))
        score = jnp.sum(jnp.maximum(sc, 0.0) * wi, axis=0, keepdims=True)
        keys_ref[j:j + 1, :] = _score_keys(score)
    kin = kin_ref[...].astype(BF16).astype(F32)
    sc = jnp.sum(qi.astype(F32) * kin, axis=1, keepdims=True)
    score = jnp.sum(jnp.maximum(sc, 0.0) * wi, axis=0, keepdims=True)
    lane = lax.broadcasted_iota(I32, (1, LANES), 1)
    keys_ref[len(pages):len(pages) + 1, :] = jnp.where(lane == 0, _score_keys(score), INT_MIN)


def _sample_scores(page_table, qi, wi, ki_new, cache_idx_k, e):
    db, n_pages = page_table.shape
    in_specs = [
        pl.BlockSpec((None, IDX_HEADS, IDX_DIM), lambda b, pt: (b, 0, 0)),
        pl.BlockSpec((None, IDX_HEADS, 1), lambda b, pt: (b, 0, 0)),
        pl.BlockSpec((None, 1, IDX_DIM), lambda b, pt: (b, 0, 0)),
    ]
    for j in range(n_pages):
        in_specs.append(pl.BlockSpec((None, None, PAGE_SIZE, IDX_DIM),
                                     lambda b, pt, j=j: (e, pt[b, j], 0, 0)))
    grid_spec = pltpu.PrefetchScalarGridSpec(
        num_scalar_prefetch=1, grid=(db,), in_specs=in_specs,
        out_specs=pl.BlockSpec((None, n_pages + 1, PAGE_SIZE), lambda b, pt: (b, 0, 0)))
    return pl.pallas_call(
        _sample_scores_kernel, grid_spec=grid_spec,
        out_shape=jax.ShapeDtypeStruct((db, n_pages + 1, PAGE_SIZE), I32),
        compiler_params=_params(("arbitrary",), 16 * 1024 * 1024), name="sample_scores",
    )(page_table, qi.reshape(db, IDX_HEADS, IDX_DIM), wi.reshape(db, IDX_HEADS, 1),
      ki_new.reshape(db, 1, IDX_DIM), *([cache_idx_k] * n_pages))


def _sample_select_kernel(keys_ref, thr_ref, cut_ref, keyst_ref, *, topk, pos_bits):
    keyst_ref[...] = pltpu.bitcast(pltpu.bitcast(keys_ref[...], F32).T, I32)
    thr_ref[...] = _threshold_and_cutoff(keyst_ref, cut_ref, keyst_ref.shape[0] // PAGE_SIZE, PAGE_SIZE,
                                         topk, pos_bits)


def _sample_select(keys, n_real):
    db, rows, _ = keys.shape
    assert db == LANES
    n_pos = rows * PAGE_SIZE
    kern = functools.partial(_sample_select_kernel, topk=min(TOPK_MAX, n_real // 4),
                             pos_bits=(n_pos - 1).bit_length())
    return pl.pallas_call(
        kern, out_shape=(jax.ShapeDtypeStruct((1, LANES), I32), jax.ShapeDtypeStruct((1, LANES), I32)),
        scratch_shapes=[pltpu.VMEM((n_pos, LANES), I32)],
        compiler_params=_params(None, 16 * 1024 * 1024), name="sample_select",
    )(keys.reshape(db, n_pos))


def _sample_attn_kernel(pt_ref, thr_ref, cut_ref, q_ref, kn_ref, vn_ref, keys_ref, *rest, n_pages):
    del pt_ref
    k_pages, v_pages, oa_ref = rest[:n_pages], rest[n_pages:2 * n_pages], rest[-1]
    b = pl.program_id(0)
    thr, cut = thr_ref[b], cut_ref[b]
    past = n_pages * PAGE_SIZE
    head = lax.broadcasted_iota(I32, (A_HEADS, A_WIDTH), 0)
    lane_w = lax.broadcasted_iota(I32, (A_HEADS, A_WIDTH), 1)
    own = (lane_w // HEAD_DIM) == head
    q = q_ref[...].astype(F32)
    q_bd32 = jnp.where(own, jnp.broadcast_to(q, (A_HEADS, A_WIDTH)), 0.0)
    q_bd = q_bd32.astype(BF16)
    slope = jnp.zeros((A_HEADS, 1), F32)
    hcol = lax.broadcasted_iota(I32, (A_HEADS, 1), 0)
    for h in range(A_HEADS):
        slope = jnp.where(hcol == h, ALIBI_SLOPES[h], slope)
    lane = lax.broadcasted_iota(I32, (1, PAGE_SIZE), 1)
    logits = []
    for j in range(n_pages):
        pos = lane + j * PAGE_SIZE
        s = _dot_nt(q_bd, k_pages[j][...].astype(BF16))
        sel = _selected(keys_ref[j:j + 1, :], pos, thr, cut)
        logits.append(jnp.where(sel, s - slope * (past - pos).astype(F32), NEG_BIG))
    kn = kn_ref[...].astype(BF16).astype(F32)
    s_new = jnp.sum(q_bd32 * kn, axis=1, keepdims=True)
    sel_new = _selected(keys_ref[n_pages:n_pages + 1, 0:1], past, thr, cut)
    logit_new = jnp.where(sel_new, s_new, NEG_BIG)
    m = logit_new
    for lg in logits:
        m = jnp.maximum(m, jnp.max(lg, axis=1, keepdims=True))
    p_new = jnp.exp(logit_new - m)
    l = p_new
    o = p_new * vn_ref[...].astype(BF16).astype(F32)
    for j in range(n_pages):
        p = jnp.exp(logits[j] - m)
        l = l + jnp.sum(p, axis=1, keepdims=True)
        o = o + _dot(p.astype(BF16), v_pages[j][...].astype(BF16))
    oa_ref[...] = jnp.sum(jnp.where(own, o / l, 0.0), axis=0, keepdims=True)


def _sample_attn(page_table, thr, cut, qs, k_new, v_new, keys, cache_k, cache_v, e):
    db, n_pages = page_table.shape
    ck = cache_k.reshape(cache_k.shape[0], cache_k.shape[1], PAGE_SIZE, A_WIDTH)
    cv = cache_v.reshape(cache_v.shape[0], cache_v.shape[1], PAGE_SIZE, A_WIDTH)
    per_b = lambda b, pt, th, cu: (b, 0, 0)
    in_specs = [
        pl.BlockSpec((None, 1, A_WIDTH), per_b),
        pl.BlockSpec((None, 1, A_WIDTH), per_b),
        pl.BlockSpec((None, 1, A_WIDTH), per_b),
        pl.BlockSpec((None, n_pages + 1, PAGE_SIZE), per_b),
    ]
    for _ in range(2):
        for j in range(n_pages):
            in_specs.append(pl.BlockSpec((None, None, PAGE_SIZE, A_WIDTH),
                                         lambda b, pt, th, cu, j=j: (e, pt[b, j], 0, 0)))
    grid_spec = pltpu.PrefetchScalarGridSpec(
        num_scalar_prefetch=3, grid=(db,), in_specs=in_specs,
        out_specs=pl.BlockSpec((None, 1, A_WIDTH), per_b))
    out = pl.pallas_call(
        functools.partial(_sample_attn_kernel, n_pages=n_pages), grid_spec=grid_spec,
        out_shape=jax.ShapeDtypeStruct((db, 1, A_WIDTH), F32),
        compiler_params=_params(("arbitrary",), 32 * 1024 * 1024), name="sample_attn",
    )(page_table, thr.reshape(db), cut.reshape(db), qs.reshape(db, 1, A_WIDTH),
      k_new.reshape(db, 1, A_WIDTH), v_new.reshape(db, 1, A_WIDTH), keys,
      *([ck] * n_pages), *([cv] * n_pages))
    return out.reshape(db, A_WIDTH)


def _even_post_kernel(x_ref, oa_ref, rest_ref, wout_ref, ws_ref, bias_ref, blng_ref, blnb_ref,
                      lng_ref, lnb_ref, y_ref, bv_ref, *, single_row):
    rest = rest_ref[...]
    ga, ub = rest[:, 0:A_WIDTH], rest[:, A_WIDTH:2 * A_WIDTH]
    vb, gb = rest[:, 2 * A_WIDTH:3 * A_WIDTH], rest[:, 3 * A_WIDTH:4 * A_WIDTH]
    vn = _layer_norm(vb, blng_ref[...], blnb_ref[...])
    bv_ref[...] = vn.reshape(bv_ref.shape)
    if single_row:
        mixed = vn * ws_ref[...] + bias_ref[...]
    else:
        row_i = lax.broadcasted_iota(I32, (CHUNK, CHUNK), 0)
        col_i = lax.broadcasted_iota(I32, (CHUNK, CHUNK), 1)
        lane = lax.broadcasted_iota(I32, (1, LANES), 1)
        vnb = vn.astype(BF16)
        parts = []
        for p in range(B_GROUPS // 2):
            vp = vnb[:, p * LANES:(p + 1) * LANES]
            w0 = jnp.where(row_i >= col_i, ws_ref[2 * p], 0.0).astype(BF16)
            w1 = jnp.where(row_i >= col_i, ws_ref[2 * p + 1], 0.0).astype(BF16)
            parts.append(jnp.where(lane < B_GDIM, _dot(w0, vp), _dot(w1, vp)))
        mixed = jnp.concatenate(parts, axis=1) + bias_ref[...]
    ob = ub * mixed * _silu(gb)
    oag = oa_ref[...] * _silu(ga)
    z = _dot(oag.astype(BF16), wout_ref[0:A_WIDTH, :]) + _dot(ob.astype(BF16), wout_ref[A_WIDTH:, :])
    y_ref[...] = _layer_norm(ALPHA * x_ref[...] + z, lng_ref[...], lnb_ref[...])


def _even_post(x2d, oa, rest, w_out, b_ws, b_bs, b_ln_g, b_ln_b, ln_g, ln_b, nb, single_row):
    n = x2d.shape[0]
    tm = CHUNK
    tpb = (n // nb) // tm if not single_row else 1
    row = lambda i: (i, 0)
    const2 = lambda i: (0, 0)
    if single_row:
        ws = jnp.repeat(b_ws[:, 0, 0], B_GDIM).reshape(1, B_WIDTH)
        bias = jnp.repeat(b_bs[:, 0], B_GDIM).reshape(1, B_WIDTH)
        ws_spec = pl.BlockSpec(ws.shape, const2)
        bv_shape = jax.ShapeDtypeStruct((n, B_WIDTH), F32)
        bv_spec = pl.BlockSpec((tm, B_WIDTH), row)
    else:
        ws = b_ws
        bias = jnp.repeat(b_bs.T, B_GDIM, axis=1)
        ws_spec = pl.BlockSpec(ws.shape, lambda i: (0, 0, 0))
        bv_shape = jax.ShapeDtypeStruct((nb, CHUNK, B_WIDTH), F32)
        bv_spec = pl.BlockSpec((1, CHUNK, B_WIDTH), lambda i: (i // tpb, 0, 0))
    in_specs = [
        pl.BlockSpec((tm, D_MODEL), row),
        pl.BlockSpec((tm, A_WIDTH), row),
        pl.BlockSpec((tm, 4 * A_WIDTH), row),
        pl.BlockSpec((A_WIDTH + B_WIDTH, D_MODEL), const2),
        ws_spec,
        pl.BlockSpec(bias.shape, const2),
        pl.BlockSpec((1, B_WIDTH), const2),
        pl.BlockSpec((1, B_WIDTH), const2),
        pl.BlockSpec((1, D_MODEL), const2),
        pl.BlockSpec((1, D_MODEL), const2),
    ]
    return pl.pallas_call(
        functools.partial(_even_post_kernel, single_row=single_row), grid=(n // tm,), in_specs=in_specs,
        out_specs=(pl.BlockSpec((tm, D_MODEL), row), bv_spec),
        out_shape=(jax.ShapeDtypeStruct((n, D_MODEL), F32), bv_shape),
        compiler_params=_params(("arbitrary",), 24 * 1024 * 1024), name="even_post",
    )(x2d, oa, rest, w_out.astype(BF16), ws, bias, b_ln_g.reshape(1, -1), b_ln_b.reshape(1, -1),
      ln_g.reshape(1, -1), ln_b.reshape(1, -1))


def _head_ones():
    r = lax.broadcasted_iota(I32, (LANES, LANES), 0) // C_HDIM
    c = lax.broadcasted_iota(I32, (LANES, LANES), 1) // C_HDIM
    return jnp.where(r == c, 1.0, 0.0).astype(BF16)


def _rwkv_pre_kernel(*refs, sample, has_vres, tpb):
    it = iter(refs)
    x_ref, xprev_ref, mu_ref, win_ref = next(it), next(it), next(it), next(it)
    w0_ref, w1_ref, w2_ref, a0_ref, a1_ref, a2_ref = (next(it) for _ in range(6))
    if has_vres:
        v0_ref, v1_ref, v2_ref, vfirst_ref = (next(it) for _ in range(4))
    kk_ref, ka_ref = next(it), next(it)
    r_ref, k_ref, v_ref, kkn_ref, a_ref, lw_ref, g_ref, vnat_ref = (next(it) for _ in range(8))

    x = x_ref[...]
    if sample:
        xp = xprev_ref[...]
    else:
        i = pl.program_id(0)
        prev_row = jnp.where(i % tpb == 0, 0.0, xprev_ref[SUBLANES - 1:SUBLANES, :])
        row_i = lax.broadcasted_iota(I32, x.shape, 0)
        xp = jnp.where(row_i == 0, prev_row, pltpu.roll(x, 1, axis=0))
    xx = xp - x
    mix = lambda j: (x + xx * mu_ref[j:j + 1, :])
    xr, xw, xk, xv, xa, xg = (mix(j) for j in range(6))
    r = _dot(xr.astype(BF16), win_ref[0])
    k = _dot(xk.astype(BF16), win_ref[1])
    v = _dot(xv.astype(BF16), win_ref[2])
    g = _dot(xg.astype(BF16), win_ref[3])
    z = w0_ref[...] + _dot(jnp.tanh(_dot(xw.astype(BF16), w1_ref[...])).astype(BF16), w2_ref[...])
    w_log = -(jnp.maximum(-z, 0.0) + jnp.log1p(jnp.exp(-jnp.abs(z)))) - 0.5
    lw = -jnp.exp(w_log)
    a = _sigmoid(a0_ref[...] + _dot(_dot(xa.astype(BF16), a1_ref[...]).astype(BF16), a2_ref[...]))
    if has_vres:
        gate = _sigmoid(v0_ref[...] + _dot(_dot(xv.astype(BF16), v1_ref[...]).astype(BF16), v2_ref[...]))
        v = v + (vfirst_ref[...] - v) * gate
    kk = k * kk_ref[...]
    ones_bd = _head_ones()
    ss = jnp.concatenate(
        [_dot_exact_rhs(jnp.square(kk[:, p * LANES:(p + 1) * LANES]), ones_bd) for p in range(N_PAIRS)], axis=1)
    kkn = kk / jnp.maximum(jnp.sqrt(ss), 1e-12)
    k2 = k * (1.0 + (a - 1.0) * ka_ref[...])
    g_ref[...] = g
    vnat_ref[...] = v
    outs = ((r_ref, r), (k_ref, k2), (v_ref, v), (kkn_ref, kkn), (a_ref, a), (lw_ref, lw))
    if sample:
        for ref, val in outs:
            ref[...] = val
    else:
        for ref, val in outs:
            for p in range(N_PAIRS):
                ref[0, p] = val[:, p * LANES:(p + 1) * LANES]


def _rwkv_pre(x2d, nb, sample, shift_prev, w, v_first):
    n = x2d.shape[0]
    t = n // nb
    tm = LANES if sample else min(128, t)
    tpb = max(t // tm, 1)
    has_vres = v_first is not None
    row = lambda i: (i, 0)
    const2 = lambda i: (0, 0)
    vec = lambda a: a.reshape(1, -1)
    args = [x2d]
    in_specs = [pl.BlockSpec((tm, D_MODEL), row)]
    if sample:
        args.append(shift_prev)
        in_specs.append(pl.BlockSpec((tm, D_MODEL), row))
    else:
        args.append(x2d)
        in_specs.append(pl.BlockSpec((SUBLANES, D_MODEL),
                                     lambda i: (jnp.maximum(i * (tm // SUBLANES) - 1, 0), 0)))
    args += [w["mu"], w["w_in"].astype(BF16), vec(w["w0"]), w["w1"].astype(BF16), w["w2"].astype(BF16),
             vec(w["a0"]), w["a1"].astype(BF16), w["a2"].astype(BF16)]
    in_specs += [pl.BlockSpec((6, D_MODEL), const2), pl.BlockSpec((4, D_MODEL, D_MODEL), lambda i: (0, 0, 0))]
    in_specs += [pl.BlockSpec(a.shape, const2) for a in args[4:]]
    if has_vres:
        extra = [vec(w["v0"]), w["v1"].astype(BF16), w["v2"].astype(BF16)]
        args += extra + [v_first]
        in_specs += [pl.BlockSpec(a.shape, const2) for a in extra] + [pl.BlockSpec((tm, D_MODEL), row)]
    args += [vec(w["kk"]), vec(w["ka"])]
    in_specs += [pl.BlockSpec((1, D_MODEL), const2)] * 2
    nat_shape = jax.ShapeDtypeStruct((n, D_MODEL), F32)
    nat_spec = pl.BlockSpec((tm, D_MODEL), row)
    if sample:
        scan_shape, scan_spec = nat_shape, nat_spec
    else:
        scan_shape = jax.ShapeDtypeStruct((nb, N_PAIRS, t, LANES), F32)
        scan_spec = pl.BlockSpec((1, N_PAIRS, tm, LANES), lambda i: (i // tpb, 0, i % tpb, 0))
    outs = pl.pallas_call(
        functools.partial(_rwkv_pre_kernel, sample=sample, has_vres=has_vres, tpb=tpb),
        grid=(n // tm,), in_specs=in_specs,
        out_specs=(scan_spec,) * 6 + (nat_spec,) * 2, out_shape=(scan_shape,) * 6 + (nat_shape,) * 2,
        compiler_params=_params(("arbitrary",), 56 * 1024 * 1024), name="rwkv_pre",
    )(*args)
    return outs[:6], outs[6], outs[7]


def _rwkv_scan_kernel(r_ref, k_ref, v_ref, kk_ref, a_ref, lw_ref, o_ref, sfin_ref, m_ref):
    c = pl.program_id(1)
    cs = RW_CHUNK
    rows = 2 * cs

    @pl.when(c == 0)
    def _():
        m_ref[...] = jnp.zeros(m_ref.shape, F32)

    ri = lax.broadcasted_iota(I32, (rows, rows), 0)
    ci = lax.broadcasted_iota(I32, (rows, rows), 1)
    same = (ri // cs) == (ci // cs)
    tr, tc = ri % cs, ci % cs
    strict = same & (tr > tc)
    incl = same & (tr >= tc)
    eye = jnp.where(ri == ci, 1.0, 0.0).astype(F32)
    tri = jnp.where(lax.broadcasted_iota(I32, (cs, cs), 0) >= lax.broadcasted_iota(I32, (cs, cs), 1),
                    1.0, 0.0).astype(BF16)
    head0 = lax.broadcasted_iota(I32, (1, LANES), 1) < C_HDIM

    def stack(y):
        return jnp.concatenate([jnp.where(head0, y, 0.0), jnp.where(head0, 0.0, y)], axis=0)

    def off_diag(s):
        return same & ((tr & s) != 0) & ((tc & s) == 0) & ((tr // (2 * s)) == (tc // (2 * s)))

    def one_pair(p):
        r, k, v = r_ref[0, p], k_ref[0, p], v_ref[0, p]
        kk, a, lw = kk_ref[0, p], a_ref[0, p], lw_ref[0, p]
        cum = _dot_exact_lhs(tri, lw)
        yield
        tot = cum[cs - 1:cs, :]
        b = kk * a
        e_neg = jnp.exp(-cum)
        e_tail = jnp.exp(tot - cum)
        kap_s = stack(kk * jnp.exp(cum - lw))
        rt_s = stack(r * jnp.exp(cum))
        v2 = _split2(stack(v))
        left = _split2(jnp.concatenate([kap_s, rt_s], axis=0))
        right = _split2(jnp.concatenate([stack(b * e_neg), stack(k * e_neg)], axis=0))
        gram = _mm3(left, right, _dot_nt)
        yield
        l_mat = jnp.where(strict, gram[:rows, :rows], 0.0)
        akk2 = _split2(jnp.where(strict, gram[:rows, rows:], 0.0))
        arb2 = _split2(jnp.where(incl, gram[rows:, :rows], 0.0))
        ark2 = _split2(jnp.where(incl, gram[rows:, rows:], 0.0))
        tinv = eye - jnp.where(off_diag(1), l_mat, 0.0)
        akkv = _mm3(akk2, v2)
        s = 2
        while s < cs:
            t2 = _split2(tinv)
            inner = _mm3(_split2(jnp.where(off_diag(s), l_mat, 0.0)), t2)
            yield
            tinv = tinv - _mm3(t2, _split2(inner))
            yield
            s *= 2
        t2 = _split2(tinv)
        sol = _mm3(t2, _split2(jnp.concatenate([kap_s, akkv], axis=1)))
        yield
        kp, u0 = sol[:, :LANES], -sol[:, LANES:]
        kp2, u02 = _split2(kp), _split2(u0)
        rp = rt_s - _mm3(arb2, kp2)
        o0 = _mm3(arb2, u02) + _mm3(ark2, v2)
        bht2 = _split2(stack(b * e_tail).T)
        kht2 = _split2(stack(k * e_tail).T)
        trans = jnp.where(ri == ci, jnp.exp(tot), 0.0) - _mm3(bht2, kp2)
        gain = _mm3(bht2, u02) + _mm3(kht2, v2)
        yield
        m2 = _split2(m_ref[p])
        o = _mm3(_split2(rp), m2) + o0
        o_ref[0, p] = o[:cs] + o[cs:]
        m_ref[p] = _mm3(_split2(trans), m2) + gain

    chains = [one_pair(p) for p in range(N_PAIRS)]
    while chains:
        chains = [c for c in chains if next(c, "done") != "done"]

    @pl.when(c == pl.num_programs(1) - 1)
    def _():
        sfin_ref[0] = m_ref[...]


def _rwkv_scan(scan_in, nb):
    t = scan_in[0].shape[2]
    nc = t // RW_CHUNK
    blk = pl.BlockSpec((1, N_PAIRS, RW_CHUNK, LANES), lambda b, c: (b, 0, c, 0))
    return pl.pallas_call(
        _rwkv_scan_kernel, grid=(nb, nc), in_specs=[blk] * 6,
        out_specs=(blk, pl.BlockSpec((1, N_PAIRS, LANES, LANES), lambda b, c: (b, 0, 0, 0))),
        out_shape=(jax.ShapeDtypeStruct((nb, N_PAIRS, t, LANES), F32),
                   jax.ShapeDtypeStruct((nb, N_PAIRS, LANES, LANES), F32)),
        scratch_shapes=[pltpu.VMEM((N_PAIRS, LANES, LANES), F32)],
        compiler_params=_params(("arbitrary", "arbitrary"), 32 * 1024 * 1024), name="rwkv_scan",
    )(*scan_in)


def _pair_state_to_heads(m):
    nb = m.shape[0]
    m = m.reshape(nb, N_PAIRS, 2, C_HDIM, 2, C_HDIM)
    heads = jnp.stack([m[:, :, 0, :, 0, :], m[:, :, 1, :, 1, :]], axis=2)
    return jnp.swapaxes(heads.reshape(nb, C_HEADS, C_HDIM, C_HDIM), -1, -2)


def _rwkv_step_kernel(s_ref, r_ref, k_ref, v_ref, kk_ref, a_ref, lw_ref, snew_ref, o_ref):
    r, k, v = r_ref[...], k_ref[...], v_ref[...]
    kk, a = kk_ref[...], a_ref[...]
    w = jnp.exp(lw_ref[...])
    nkk = -kk
    b = kk * a
    eye = (lax.broadcasted_iota(I32, (C_HDIM, C_HDIM), 0) == lax.broadcasted_iota(I32, (C_HDIM, C_HDIM), 1))
    heads = range(C_HEADS)
    lanes = [slice(h * C_HDIM, (h + 1) * C_HDIM) for h in heads]
    s = [s_ref[h] for h in heads]
    sa = [jnp.sum(s[h] * nkk[:, lanes[h]], axis=1, keepdims=True) for h in heads]
    v_col = [jnp.sum(jnp.where(eye, v[:, lanes[h]], 0.0), axis=1, keepdims=True) for h in heads]
    s_n = [s[h] * w[:, lanes[h]] + sa[h] * b[:, lanes[h]] + v_col[h] * k[:, lanes[h]] for h in heads]
    for h in heads:
        snew_ref[h] = s_n[h]
    out_col = [jnp.sum(s_n[h] * r[:, lanes[h]], axis=1, keepdims=True) for h in heads]
    out_rows = [jnp.sum(jnp.where(eye, out_col[h], 0.0), axis=0, keepdims=True) for h in heads]
    o_ref[...] = jnp.concatenate(out_rows, axis=1)


def _rwkv_step(states, o, scan_in):
    db = states.shape[1]
    vec = pl.BlockSpec((None, 1, D_MODEL), lambda b: (b, 0, 0))
    st_in = pl.BlockSpec((None, None, C_HEADS, C_HDIM, C_HDIM), lambda b: (o, b, 0, 0, 0))
    st = pl.BlockSpec((None, C_HEADS, C_HDIM, C_HDIM), lambda b: (b, 0, 0, 0))
    new, out = pl.pallas_call(
        _rwkv_step_kernel, grid=(db,), in_specs=[st_in] + [vec] * 6, out_specs=(st, vec),
        out_shape=(jax.ShapeDtypeStruct(states.shape[1:], F32), jax.ShapeDtypeStruct((db, 1, D_MODEL), F32)),
        compiler_params=_params(("arbitrary",), 16 * 1024 * 1024), name="rwkv_step",
    )(states, *[a.reshape(db, 1, D_MODEL) for a in scan_in])
    return new, out.reshape(db, D_MODEL)


def _rwkv_post_kernel(x_ref, o_ref, r_ref, k_ref, v_ref, g_ref, rk_ref, lnxg_ref, lnxb_ref, wout_ref,
                      lng_ref, lnb_ref, y_ref, *, sample):
    if sample:
        get = lambda p: tuple(ref[:, p * LANES:(p + 1) * LANES] for ref in (o_ref, r_ref, k_ref, v_ref))
    else:
        get = lambda p: (o_ref[0, p], r_ref[0, p], k_ref[0, p], v_ref[0, p])
    ones_bd = _head_ones()
    parts = []
    for p in range(N_PAIRS):
        o, r, k, v = get(p)
        lanes = slice(p * LANES, (p + 1) * LANES)
        mean = _dot_exact_rhs(o, ones_bd) * (1.0 / C_HDIM)
        d = o - mean
        var = _dot_exact_rhs(d * d, ones_bd) * (1.0 / C_HDIM)
        on = d * lax.rsqrt(var + GN_EPS) * lnxg_ref[:, lanes] + lnxb_ref[:, lanes]
        bonus = _dot_exact_rhs(r * k * rk_ref[:, lanes], ones_bd) * v
        parts.append(on + bonus)
    out = jnp.concatenate(parts, axis=1) * _silu(g_ref[...])
    z = _dot(out.astype(BF16), wout_ref[...])
    y_ref[...] = _layer_norm(ALPHA * x_ref[...] + z, lng_ref[...], lnb_ref[...])


def _rwkv_post(x2d, o, scan_in, g, w, ln_g, ln_b, nb, sample):
    n = x2d.shape[0]
    t = n // nb
    tm = LANES if sample else min(256, t)
    tpb = max(t // tm, 1)
    row = lambda i: (i, 0)
    const2 = lambda i: (0, 0)
    vec = lambda a: a.reshape(1, -1)
    if sample:
        scan_spec = pl.BlockSpec((tm, D_MODEL), row)
    else:
        scan_spec = pl.BlockSpec((1, N_PAIRS, tm, LANES), lambda i: (i // tpb, 0, i % tpb, 0))
    r, k, v = scan_in[0], scan_in[1], scan_in[2]
    in_specs = [pl.BlockSpec((tm, D_MODEL), row), scan_spec, scan_spec, scan_spec, scan_spec,
                pl.BlockSpec((tm, D_MODEL), row)]
    in_specs += [pl.BlockSpec((1, D_MODEL), const2)] * 3 + [pl.BlockSpec((D_MODEL, D_MODEL), const2)]
    in_specs += [pl.BlockSpec((1, D_MODEL), const2)] * 2
    return pl.pallas_call(
        functools.partial(_rwkv_post_kernel, sample=sample), grid=(n // tm,), in_specs=in_specs,
        out_specs=pl.BlockSpec((tm, D_MODEL), row), out_shape=jax.ShapeDtypeStruct((n, D_MODEL), F32),
        compiler_params=_params(("arbitrary",), 32 * 1024 * 1024), name="rwkv_post",
    )(x2d, o, r, k, v, g, vec(w["rk"]), vec(w["lnx_g"]), vec(w["lnx_b"]), w["w_out"].astype(BF16),
      vec(ln_g), vec(ln_b))


def kernel(x_prompt, x_sample, cache_k, cache_v, cache_idx_k, state_wkv, state_shift, page_table, ln_g, ln_b, e_w_in, e_w_out, b_ws, b_bs, b_ln_g, b_ln_b, c_mu, c_w_in, c_w0, c_w1, c_w2, c_a0, c_a1, c_a2, c_v0, c_v1, c_v2, c_kk, c_ka, c_rk, c_lnx_g, c_lnx_b, c_w_out):
    bp, seq, _ = x_prompt.shape
    db, dec_seq, _ = x_sample.shape
    assert dec_seq == 1 and db == LANES and seq % KT == 0
    n_pages = page_table.shape[1]
    past = n_pages * PAGE_SIZE

    def odd_weights(o):
        w = dict(mu=c_mu[o], w_in=c_w_in[o], w0=c_w0[o], w1=c_w1[o], w2=c_w2[o], a0=c_a0[o], a1=c_a1[o],
                 a2=c_a2[o], kk=c_kk[o], ka=c_ka[o], rk=c_rk[o], lnx_g=c_lnx_g[o], lnx_b=c_lnx_b[o],
                 w_out=c_w_out[o])
        if o > 0:
            w.update(v0=c_v0[o - 1], v1=c_v1[o - 1], v2=c_v2[o - 1])
        return w

    x = x_prompt.reshape(bp * seq, D_MODEL)
    p_k, p_v, p_ik, p_bv, p_wkv, p_shift = [], [], [], [], [], []
    v_first = None
    for l in range(DEPTH):
        if l % 2 == 0:
            e = l // 2
            qs, k, kb, v, vt, qi, ki, kib, wit, rest = _even_proj(x, e_w_in[e], bp)
            oa = _attn_prompt(qs, qi, wit, kb, vt, kib, bp)
            x, bv = _even_post(x, oa, rest, e_w_out[e], b_ws[e], b_bs[e], b_ln_g[e], b_ln_b[e],
                               ln_g[l], ln_b[l], bp, single_row=False)
            p_k.append(k.reshape(bp, seq, A_HEADS, HEAD_DIM))
            p_v.append(v.reshape(bp, seq, A_HEADS, HEAD_DIM))
            p_ik.append(ki.reshape(bp, seq, IDX_DIM))
            p_bv.append(bv)
        else:
            o = l // 2
            w = odd_weights(o)
            p_shift.append(x.reshape(bp, seq, D_MODEL)[:, -1])
            scan_in, g, v_nat = _rwkv_pre(x, bp, False, None, w, v_first)
            if o == 0:
                v_first = v_nat
            out, m_fin = _rwkv_scan(scan_in, bp)
            p_wkv.append(_pair_state_to_heads(m_fin))
            x = _rwkv_post(x, out, scan_in, g, w, ln_g[l], ln_b[l], bp, sample=False)
    y_prompt = x.reshape(bp, seq, D_MODEL)

    x = x_sample.reshape(db, D_MODEL)
    s_k, s_v, s_ik, s_bv, s_wkv, s_shift = [], [], [], [], [], []
    v_first = None
    for l in range(DEPTH):
        if l % 2 == 0:
            e = l // 2
            qs, k, kb, v, vt, qi, ki, kib, wit, rest = _even_proj(x, e_w_in[e], 1)
            keys = _sample_scores(page_table, qi, wit[0].T, ki, cache_idx_k, e)
            thr, cut = _sample_select(keys, past + 1)
            oa = _sample_attn(page_table, thr, cut, qs, k, v, keys, cache_k, cache_v, e)
            x, bv = _even_post(x, oa, rest, e_w_out[e], b_ws[e], b_bs[e], b_ln_g[e], b_ln_b[e],
                               ln_g[l], ln_b[l], 1, single_row=True)
            s_k.append(k.reshape(db, 1, A_HEADS, HEAD_DIM))
            s_v.append(v.reshape(db, 1, A_HEADS, HEAD_DIM))
            s_ik.append(ki.reshape(db, 1, IDX_DIM))
            s_bv.append(bv.reshape(db, 1, B_WIDTH))
        else:
            o = l // 2
            w = odd_weights(o)
            s_shift.append(x)
            scan_in, g, v_nat = _rwkv_pre(x, 1, True, state_shift[o], w, v_first)
            if o == 0:
                v_first = v_nat
            new_state, out = _rwkv_step(state_wkv, o, scan_in)
            s_wkv.append(new_state)
            x = _rwkv_post(x, out, scan_in, g, w, ln_g[l], ln_b[l], 1, sample=True)
    y_sample = x.reshape(db, 1, D_MODEL)

    return (y_prompt, y_sample, jnp.stack(p_k), jnp.stack(p_v), jnp.stack(p_ik), jnp.stack(p_bv),
            jnp.stack(p_wkv), jnp.stack(p_shift), jnp.stack(s_k), jnp.stack(s_v), jnp.stack(s_ik),
            jnp.stack(s_bv), jnp.stack(s_wkv), jnp.stack(s_shift))
```

```python
import functools

import jax
import jax.numpy as jnp
from jax import lax
from jax.experimental import pallas as pl
from jax.experimental.pallas import tpu as pltpu

F32, BF16, I32 = jnp.float32, jnp.bfloat16, jnp.int32

D_MODEL = 1024
DEPTH = 4
HEAD_DIM = 64
A_HEADS = 8
A_WIDTH = A_HEADS * HEAD_DIM
IDX_HEADS = 8
IDX_DIM = 64
TOPK_MAX = 256
B_GROUPS = 8
B_GDIM = 64
B_WIDTH = B_GROUPS * B_GDIM
CHUNK = 128
C_HEADS = 16
C_HDIM = 64
PAGE_SIZE = 128
ALPHA = (2 * DEPTH) ** 0.25
LN_EPS = 1e-5
GN_EPS = 64e-5
_OFF_KI = 2048
_OFF_WI = 2112
_OFF_REST = 2120
ALIBI_SLOPES = tuple(2.0 ** (-8.0 * (h + 1.0) / A_HEADS) for h in range(A_HEADS))

LANES = 128
SUBLANES = 8
V7X_VMEM_BYTES = 64 * 1024 * 1024

QT = 128
KT = 512
VT_BLOCK = 256
RW_CHUNK = 64
PAIR = 2 * C_HDIM
N_PAIRS = C_HEADS // 2
INT_MIN = -2 ** 31
NEG_BIG = -1e30


def _dot(a, b):
    return jnp.dot(a, b, preferred_element_type=F32)


def _dot_nt(a, b):
    return lax.dot_general(a, b, (((1,), (1,)), ((), ())), preferred_element_type=F32)


def _split2(x):
    hi = x.astype(BF16)
    return hi, (x - hi.astype(F32)).astype(BF16)


def _split3(x):
    hi = x.astype(BF16)
    r1 = x - hi.astype(F32)
    mid = r1.astype(BF16)
    return hi, mid, (r1 - mid.astype(F32)).astype(BF16)


def _mm3(a2, b2, nt=False):
    dot = _dot_nt if nt else _dot
    return dot(a2[0], b2[0]) + dot(a2[1], b2[0]) + dot(a2[0], b2[1])


def _dot_exact_rhs(x, a_bf16, nt=False):
    dot = _dot_nt if nt else _dot
    hi, mid, lo = _split3(x)
    return dot(hi, a_bf16) + dot(mid, a_bf16) + dot(lo, a_bf16)


def _dot_exact_lhs(a_bf16, x):
    hi, mid, lo = _split3(x)
    return _dot(a_bf16, hi) + _dot(a_bf16, mid) + _dot(a_bf16, lo)


def _layer_norm(x, g, b, eps=LN_EPS):
    mu = jnp.mean(x, axis=-1, keepdims=True)
    d = x - mu
    var = jnp.mean(d * d, axis=-1, keepdims=True)
    return d * lax.rsqrt(var + eps) * g + b


def _sigmoid(x):
    return 1.0 / (1.0 + jnp.exp(-x))


def _silu(x):
    return x * _sigmoid(x)


def _params(semantics, vmem_bytes):
    assert vmem_bytes <= V7X_VMEM_BYTES
    return pltpu.CompilerParams(dimension_semantics=semantics, vmem_limit_bytes=vmem_bytes)


def _even_proj_kernel(x_ref, wa_ref, widx_ref, wwit_ref, wvt_ref, wr_ref,
                      qs_ref, k_ref, kb_ref, v_ref, vt_ref, qi_ref, ki_ref, ki3_ref, wit_ref, rest_ref):
    xf = x_ref[...]
    x = xf.astype(BF16)
    h = _dot(x, wa_ref[...])
    qs_ref[...] = (h[:, 0:A_WIDTH] * (HEAD_DIM ** -0.5)).astype(BF16)
    k = h[:, A_WIDTH:2 * A_WIDTH]
    k_ref[...] = k
    kb_ref[...] = k.astype(BF16)
    v_ref[...] = h[:, 2 * A_WIDTH:3 * A_WIDTH]
    x2 = _split2(xf)
    hidx = _mm3(x2, _split2(widx_ref[...]))
    qi_ref[...] = hidx[:, :IDX_HEADS * IDX_DIM] * (IDX_DIM ** -0.5)
    ki = hidx[:, IDX_HEADS * IDX_DIM:]
    ki_ref[...] = ki
    kh, kl = _split2(ki)
    kh = kh.astype(F32)
    ki3_ref[...] = jnp.concatenate([kh, kl.astype(F32), kh], axis=1).astype(BF16)
    wit_ref[0] = _mm3(_split2(wwit_ref[...]), x2, nt=True) * (IDX_HEADS ** -0.5)
    vt = _dot_nt(wvt_ref[...], x)
    kt = vt_ref.shape[3]
    for j in range(vt_ref.shape[1]):
        vt_ref[0, j] = vt[:, j * kt:(j + 1) * kt].astype(BF16)
    rest_ref[...] = _dot(x, wr_ref[...])


def _even_proj(x2d, w_in, nb):
    n = x2d.shape[0]
    t = n // nb
    tm = min(256, t)
    tpb = t // tm
    kt = min(VT_BLOCK, t)
    wa = w_in[:, :3 * A_WIDTH].astype(BF16)
    widx = w_in[:, 3 * A_WIDTH:_OFF_WI]
    wwit = w_in[:, _OFF_WI:_OFF_REST].T
    wvt = w_in[:, 2 * A_WIDTH:3 * A_WIDTH].T.astype(BF16)
    wr = w_in[:, _OFF_REST:].astype(BF16)
    row = lambda i: (i, 0)
    const = lambda i: (0, 0)
    out_shape = (
        jax.ShapeDtypeStruct((n, A_WIDTH), BF16),
        jax.ShapeDtypeStruct((n, A_WIDTH), F32),
        jax.ShapeDtypeStruct((n, A_WIDTH), BF16),
        jax.ShapeDtypeStruct((n, A_WIDTH), F32),
        jax.ShapeDtypeStruct((nb, t // kt, A_WIDTH, kt), BF16),
        jax.ShapeDtypeStruct((n, IDX_HEADS * IDX_DIM), F32),
        jax.ShapeDtypeStruct((n, IDX_DIM), F32),
        jax.ShapeDtypeStruct((n, 3 * IDX_DIM), BF16),
        jax.ShapeDtypeStruct((nb, IDX_HEADS, t), F32),
        jax.ShapeDtypeStruct((n, 4 * A_WIDTH), F32),
    )
    out_specs = (
        pl.BlockSpec((tm, A_WIDTH), row),
        pl.BlockSpec((tm, A_WIDTH), row),
        pl.BlockSpec((tm, A_WIDTH), row),
        pl.BlockSpec((tm, A_WIDTH), row),
        pl.BlockSpec((1, tm // kt, A_WIDTH, kt), lambda i: (i // tpb, i % tpb, 0, 0)),
        pl.BlockSpec((tm, IDX_HEADS * IDX_DIM), row),
        pl.BlockSpec((tm, IDX_DIM), row),
        pl.BlockSpec((tm, 3 * IDX_DIM), row),
        pl.BlockSpec((1, IDX_HEADS, tm), lambda i: (i // tpb, 0, i % tpb)),
        pl.BlockSpec((tm, 4 * A_WIDTH), row),
    )
    in_specs = [
        pl.BlockSpec((tm, D_MODEL), row),
        pl.BlockSpec(wa.shape, const),
        pl.BlockSpec(widx.shape, const),
        pl.BlockSpec(wwit.shape, const),
        pl.BlockSpec(wvt.shape, const),
        pl.BlockSpec(wr.shape, const),
    ]
    return pl.pallas_call(
        _even_proj_kernel, grid=(n // tm,), in_specs=in_specs, out_specs=out_specs, out_shape=out_shape,
        compiler_params=_params(("arbitrary",), 48 * 1024 * 1024), name="even_proj",
    )(x2d, wa, widx, wwit, wvt, wr)


def _score_keys(score):
    bits = pltpu.bitcast(score + 0.0, I32)
    return jnp.where(bits < 0, bits ^ jnp.int32(0x7FFFFFFF), bits)


def _count(keys_ref, nkb, kt, pred):
    row_i = lax.broadcasted_iota(I32, (kt, LANES), 0)

    def body(kb, acc):
        start = pl.multiple_of(kb * kt, kt)
        blk = keys_ref[pl.ds(start, kt), :]
        hit = jnp.where(pred(blk, row_i + start), 1, 0).astype(I32)
        return acc + jnp.sum(hit.reshape(kt // SUBLANES, SUBLANES, LANES), axis=0)

    acc = lax.fori_loop(0, nkb, body, jnp.zeros((SUBLANES, LANES), I32))
    return jnp.sum(acc, axis=0, keepdims=True)


def _select_threshold(keys_ref, nkb, kt, topk):
    def bit_body(j, thr):
        cand = thr + lax.shift_left(jnp.int32(1), 31 - j)
        cnt = _count(keys_ref, nkb, kt, lambda blk, _: blk >= cand)
        return jnp.where(cnt >= topk, cand, thr)

    thr = lax.fori_loop(0, 32, bit_body, jnp.full((1, LANES), INT_MIN, I32))
    cnt_gt = _count(keys_ref, nkb, kt, lambda blk, _: blk > thr)
    cnt_ge = _count(keys_ref, nkb, kt, lambda blk, _: blk >= thr)
    need = topk - cnt_gt
    tie = ((cnt_ge - cnt_gt) > need) & (thr > INT_MIN)
    return thr, need, tie


def _tie_cutoff(keys_ref, nkb, kt, thr, need, pos_bits):
    def bit_body(j, c):
        cand = c + lax.shift_left(jnp.int32(1), pos_bits - 1 - j)
        before = _count(keys_ref, nkb, kt, lambda blk, pos: (blk == thr) & (pos < cand))
        return jnp.where(before < need, cand, c)

    return lax.fori_loop(0, pos_bits, bit_body, jnp.zeros((1, LANES), I32))


def _threshold_and_cutoff(keys_ref, cut_ref, nkb, kt, topk, pos_bits):
    thr, need, tie = _select_threshold(keys_ref, nkb, kt, topk)
    cut_ref[...] = jnp.full((1, LANES), 2 ** pos_bits - 1, I32)

    @pl.when(jnp.max(tie.astype(I32)) > 0)
    def _():
        cut_ref[...] = _tie_cutoff(keys_ref, nkb, kt, thr, need, pos_bits)

    return thr


def _selected(key, pos, thr, cut):
    return (key > thr) | ((key == thr) & (pos <= cut))


def _attn_prompt_kernel(qs_ref, qi_ref, wit_ref, kb_ref, vt_ref, kib_ref, oa_ref,
                        keys_ref, acc_ref, cut_ref, s_ref, p_ref, bias_ref, *, topk, pos_bits):
    i = pl.program_id(1)
    nkb = (i * QT + QT + KT - 1) // KT
    row_i = lax.broadcasted_iota(I32, (KT, QT), 0)
    col_i = lax.broadcasted_iota(I32, (KT, QT), 1)
    tpos = col_i + i * QT

    qi = qi_ref[...]
    qi_h = [qi[:, h * IDX_DIM:(h + 1) * IDX_DIM] for h in range(IDX_HEADS)]
    qi_two = []
    for j in range(IDX_HEADS // 2):
        q_hi, q_lo = _split2(jnp.concatenate([qi_h[2 * j], qi_h[2 * j + 1]], axis=0))
        q_hi = q_hi.astype(F32)
        qi_two.append(jnp.concatenate([q_hi, q_hi, q_lo.astype(F32)], axis=1).astype(BF16))
    wit = wit_ref[0]

    def score_body(kb, carry):
        start = pl.multiple_of(kb * KT, KT)
        kib = kib_ref[pl.ds(start, KT), :]
        for j in range(IDX_HEADS // 2):
            two = _dot_nt(kib, qi_two[j])
            s_ref[2 * j] = two[:, :QT]
            s_ref[2 * j + 1] = two[:, QT:]
        score = jnp.zeros((KT, QT), F32)
        for h in range(IDX_HEADS):
            score = score + jnp.maximum(s_ref[h], 0.0) * wit[h:h + 1, :]
        key = jnp.where(row_i + start <= tpos, _score_keys(score), INT_MIN)
        keys_ref[pl.ds(start, KT), :] = key
        return carry

    lax.fori_loop(0, nkb, score_body, 0)
    thr = _threshold_and_cutoff(keys_ref, cut_ref, nkb, KT, topk, pos_bits)
    cut = cut_ref[...]

    q = qs_ref[...]
    lane = lax.broadcasted_iota(I32, (1, LANES), 1)
    q_pad = []
    for h in range(A_HEADS):
        pair = q[:, (h // 2) * LANES:(h // 2 + 1) * LANES].astype(F32)
        q_pad.append(jnp.where((lane // HEAD_DIM) == (h % 2), pair, 0.0).astype(BF16))
    q_two = [jnp.concatenate([q_pad[2 * j], q_pad[2 * j + 1]], axis=0) for j in range(A_HEADS // 2)]
    acc_ref[...] = jnp.zeros(acc_ref.shape, F32)
    row_f = row_i.astype(F32)
    for h in range(A_HEADS):
        bias_ref[h] = ALIBI_SLOPES[h] * row_f

    def kv_body(kb, carry):
        m_all, l_all = carry
        start = pl.multiple_of(kb * KT, KT)
        spos = row_i + start
        key = keys_ref[pl.ds(start, KT), :]
        sel = _selected(key, spos, thr, cut) & (spos <= tpos)
        trip_off = (i * QT - start).astype(F32)
        m_out, l_out, alphas = [], [], []
        for j in range(A_HEADS // 2):
            kp = kb_ref[pl.ds(start, KT), j * LANES:(j + 1) * LANES]
            two = _dot_nt(kp, q_two[j])
            s_ref[2 * j] = two[:, :QT]
            s_ref[2 * j + 1] = two[:, QT:]
        for h in range(A_HEADS):
            logit = jnp.where(sel, s_ref[h] + bias_ref[h], NEG_BIG)
            shift = ALIBI_SLOPES[h] * trip_off
            m_new = jnp.maximum(m_all[h], jnp.max(logit, axis=0, keepdims=True) - shift)
            p = jnp.exp(logit - (m_new + shift))
            alpha = jnp.exp(m_all[h] - m_new)
            l_out.append(alpha * l_all[h] + jnp.sum(p, axis=0, keepdims=True))
            m_out.append(m_new)
            alphas.append(alpha)
            p_ref[h] = p.astype(BF16)
        for h in range(A_HEADS):
            hrows = slice(h * HEAD_DIM, (h + 1) * HEAD_DIM)
            per_trip = KT // VT_BLOCK
            vt = jnp.concatenate([vt_ref[0, kb * per_trip + j, hrows, :] for j in range(per_trip)], axis=1)
            rows = slice(h * HEAD_DIM, (h + 1) * HEAD_DIM)
            acc_ref[rows, :] = alphas[h] * acc_ref[rows, :] + _dot(vt, p_ref[h])
        return tuple(m_out), tuple(l_out)

    init = (tuple(jnp.full((1, QT), NEG_BIG, F32) for _ in range(A_HEADS)),
            tuple(jnp.zeros((1, QT), F32) for _ in range(A_HEADS)))
    _, l_fin = lax.fori_loop(0, nkb, kv_body, init)
    for h in range(A_HEADS):
        rows = slice(h * HEAD_DIM, (h + 1) * HEAD_DIM)
        acc_ref[rows, :] = acc_ref[rows, :] / l_fin[h]
    oa_ref[...] = acc_ref[...].T


def _attn_prompt(qs, qi, wit, kb, vt, kib, nb):
    n = qs.shape[0]
    t = n // nb
    nq = t // QT
    topk = min(TOPK_MAX, t // 4)
    pos_bits = max(1, (t - 1).bit_length())
    once = pl.Buffered(1)
    in_specs = [
        pl.BlockSpec((QT, A_WIDTH), lambda b, i: (b * nq + i, 0)),
        pl.BlockSpec((QT, IDX_HEADS * IDX_DIM), lambda b, i: (b * nq + i, 0)),
        pl.BlockSpec((1, IDX_HEADS, QT), lambda b, i: (b, 0, i)),
        pl.BlockSpec((t, A_WIDTH), lambda b, i: (b, 0), pipeline_mode=once),
        pl.BlockSpec((1, t // VT_BLOCK, A_WIDTH, VT_BLOCK), lambda b, i: (b, 0, 0, 0), pipeline_mode=once),
        pl.BlockSpec((t, 3 * IDX_DIM), lambda b, i: (b, 0), pipeline_mode=once),
    ]
    scratch = [
        pltpu.VMEM((t, QT), I32),
        pltpu.VMEM((A_WIDTH, QT), F32),
        pltpu.VMEM((1, QT), I32),
        pltpu.VMEM((A_HEADS, KT, QT), F32),
        pltpu.VMEM((A_HEADS, KT, QT), BF16),
        pltpu.VMEM((A_HEADS, KT, QT), F32),
    ]
    kern = functools.partial(_attn_prompt_kernel, topk=topk, pos_bits=pos_bits)
    return pl.pallas_call(
        kern, grid=(nb, nq), in_specs=in_specs,
        out_specs=pl.BlockSpec((QT, A_WIDTH), lambda b, i: (b * nq + i, 0)),
        out_shape=jax.ShapeDtypeStruct((n, A_WIDTH), F32), scratch_shapes=scratch,
        compiler_params=_params(("arbitrary", "arbitrary"), 40 * 1024 * 1024), name="attn_prompt",
    )(qs, qi, wit, kb, vt, kib)


def _sample_scores_kernel(pt_ref, qi_ref, wi_ref, kin_ref, *rest):
    del pt_ref
    pages, keys_ref = rest[:-1], rest[-1]
    qi = qi_ref[...]
    wi = wi_ref[...]
    for j, page in enumerate(pages):
        sc = _dot_nt(qi, page[...].astype(BF16))
        score = jnp.sum(jnp.maximum(sc, 0.0) * wi, axis=0, keepdims=True)
        keys_ref[j:j + 1, :] = _score_keys(score)
    kin = kin_ref[...].astype(BF16).astype(F32)
    sc = jnp.sum(qi.astype(F32) * kin, axis=1, keepdims=True)
    score = jnp.sum(jnp.maximum(sc, 0.0) * wi, axis=0, keepdims=True)
    lane = lax.broadcasted_iota(I32, (1, LANES), 1)
    keys_ref[len(pages):len(pages) + 1, :] = jnp.where(lane == 0, _score_keys(score), INT_MIN)


def _sample_scores(page_table, qi, wi, ki_new, cache_idx_k, e):
    db, n_pages = page_table.shape
    in_specs = [
        pl.BlockSpec((None, IDX_HEADS, IDX_DIM), lambda b, pt: (b, 0, 0)),
        pl.BlockSpec((None, IDX_HEADS, 1), lambda b, pt: (b, 0, 0)),
        pl.BlockSpec((None, 1, IDX_DIM), lambda b, pt: (b, 0, 0)),
    ]
    for j in range(n_pages):
        in_specs.append(pl.BlockSpec((None, None, PAGE_SIZE, IDX_DIM),
                                     lambda b, pt, j=j: (e, pt[b, j], 0, 0)))
    grid_spec = pltpu.PrefetchScalarGridSpec(
        num_scalar_prefetch=1, grid=(db,), in_specs=in_specs,
        out_specs=pl.BlockSpec((None, n_pages + 1, PAGE_SIZE), lambda b, pt: (b, 0, 0)))
    return pl.pallas_call(
        _sample_scores_kernel, grid_spec=grid_spec,
        out_shape=jax.ShapeDtypeStruct((db, n_pages + 1, PAGE_SIZE), I32),
        compiler_params=_params(("arbitrary",), 16 * 1024 * 1024), name="sample_scores",
    )(page_table, qi.reshape(db, IDX_HEADS, IDX_DIM), wi.reshape(db, IDX_HEADS, 1),
      ki_new.reshape(db, 1, IDX_DIM), *([cache_idx_k] * n_pages))


def _sample_select_kernel(keys_ref, thr_ref, cut_ref, keyst_ref, *, topk, pos_bits):
    keyst_ref[...] = pltpu.bitcast(pltpu.bitcast(keys_ref[...], F32).T, I32)
    thr_ref[...] = _threshold_and_cutoff(keyst_ref, cut_ref, keyst_ref.shape[0] // PAGE_SIZE, PAGE_SIZE,
                                         topk, pos_bits)


def _sample_select(keys, n_real):
    db, rows, _ = keys.shape
    assert db == LANES
    n_pos = rows * PAGE_SIZE
    kern = functools.partial(_sample_select_kernel, topk=min(TOPK_MAX, n_real // 4),
                             pos_bits=(n_pos - 1).bit_length())
    return pl.pallas_call(
        kern, out_shape=(jax.ShapeDtypeStruct((1, LANES), I32), jax.ShapeDtypeStruct((1, LANES), I32)),
        scratch_shapes=[pltpu.VMEM((n_pos, LANES), I32)],
        compiler_params=_params(None, 16 * 1024 * 1024), name="sample_select",
    )(keys.reshape(db, n_pos))


def _sample_attn_kernel(pt_ref, thr_ref, cut_ref, q_ref, kn_ref, vn_ref, keys_ref, exp_ref, *rest, n_pages):
    del pt_ref
    k_pages, v_pages, oa_ref = rest[:n_pages], rest[n_pages:2 * n_pages], rest[-1]
    b = pl.program_id(0)
    thr, cut = thr_ref[b], cut_ref[b]
    past = n_pages * PAGE_SIZE
    wide = PAGE_SIZE * A_HEADS
    own = (lax.broadcasted_iota(I32, (A_HEADS, wide), 1) % A_HEADS) == lax.broadcasted_iota(I32, (A_HEADS, wide), 0)
    expand = exp_ref[...]
    q = q_ref[...]
    slope = jnp.zeros((A_HEADS, 1), F32)
    hcol = lax.broadcasted_iota(I32, (A_HEADS, 1), 0)
    for h in range(A_HEADS):
        slope = jnp.where(hcol == h, ALIBI_SLOPES[h], slope)
    lane = lax.broadcasted_iota(I32, (1, PAGE_SIZE), 1)
    pages = range(n_pages)
    s_wide = [_dot_nt(q, k_pages[j][...].astype(BF16)) for j in pages]
    s_all = _dot_exact_rhs(jnp.concatenate([jnp.where(own, s_wide[j], 0.0) for j in pages], axis=0), expand, nt=True)
    s_own = [s_all[j * A_HEADS:(j + 1) * A_HEADS] for j in pages]
    logits = []
    for j in pages:
        pos = lane + j * PAGE_SIZE
        sel = _selected(keys_ref[j:j + 1, :], pos, thr, cut)
        logits.append(jnp.where(sel, s_own[j] - slope * (past - pos).astype(F32), NEG_BIG))
    kn = kn_ref[...].astype(BF16).astype(F32)
    s_new = jnp.sum(q.astype(F32) * kn, axis=1, keepdims=True)
    sel_new = _selected(keys_ref[n_pages:n_pages + 1, 0:1], past, thr, cut)
    logit_new = jnp.where(sel_new, s_new, NEG_BIG)
    m = logit_new
    for lg in logits:
        m = jnp.maximum(m, jnp.max(lg, axis=1, keepdims=True))
    p_new = jnp.exp(logit_new - m)
    l = p_new
    o = p_new * vn_ref[...].astype(BF16).astype(F32)
    p = [jnp.exp(logits[j] - m) for j in pages]
    for j in pages:
        l = l + jnp.sum(p[j], axis=1, keepdims=True)
    pw_all = _dot(jnp.concatenate(p, axis=0).astype(BF16), expand)
    p_wide = [jnp.where(own, pw_all[j * A_HEADS:(j + 1) * A_HEADS], 0.0).astype(BF16) for j in pages]
    pv = [_dot(p_wide[j], v_pages[j][...].astype(BF16)) for j in pages]
    for j in pages:
        o = o + pv[j]
    oa_ref[...] = o / l


def _sample_attn(page_table, thr, cut, qs, k_new, v_new, keys, cache_k, cache_v, e):
    db, n_pages = page_table.shape
    rows = PAGE_SIZE * A_HEADS
    ck = cache_k.reshape(cache_k.shape[0], cache_k.shape[1], rows, HEAD_DIM)
    cv = cache_v.reshape(cache_v.shape[0], cache_v.shape[1], rows, HEAD_DIM)
    expand = jnp.repeat(jnp.eye(PAGE_SIZE, dtype=BF16), A_HEADS, axis=1)
    per_b = lambda b, pt, th, cu: (b, 0, 0)
    head_rows = pl.BlockSpec((None, A_HEADS, HEAD_DIM), per_b)
    in_specs = [head_rows, head_rows, head_rows,
                pl.BlockSpec((None, n_pages + 1, PAGE_SIZE), per_b),
                pl.BlockSpec(expand.shape, lambda b, pt, th, cu: (0, 0))]
    for _ in range(2):
        for j in range(n_pages):
            in_specs.append(pl.BlockSpec((None, None, rows, HEAD_DIM),
                                         lambda b, pt, th, cu, j=j: (e, pt[b, j], 0, 0)))
    grid_spec = pltpu.PrefetchScalarGridSpec(
        num_scalar_prefetch=3, grid=(db,), in_specs=in_specs, out_specs=head_rows)
    out = pl.pallas_call(
        functools.partial(_sample_attn_kernel, n_pages=n_pages), grid_spec=grid_spec,
        out_shape=jax.ShapeDtypeStruct((db, A_HEADS, HEAD_DIM), F32),
        compiler_params=_params(("arbitrary",), 48 * 1024 * 1024), name="sample_attn",
    )(page_table, thr.reshape(db), cut.reshape(db), qs.reshape(db, A_HEADS, HEAD_DIM),
      k_new.reshape(db, A_HEADS, HEAD_DIM), v_new.reshape(db, A_HEADS, HEAD_DIM), keys, expand,
      *([ck] * n_pages), *([cv] * n_pages))
    return out.reshape(db, A_WIDTH)


def _even_post_kernel(x_ref, oa_ref, rest_ref, wout_ref, ws_ref, bias_ref, blng_ref, blnb_ref,
                      lng_ref, lnb_ref, y_ref, bv_ref, *, single_row):
    rest = rest_ref[...]
    ga, ub = rest[:, 0:A_WIDTH], rest[:, A_WIDTH:2 * A_WIDTH]
    vb, gb = rest[:, 2 * A_WIDTH:3 * A_WIDTH], rest[:, 3 * A_WIDTH:4 * A_WIDTH]
    vn = _layer_norm(vb, blng_ref[...], blnb_ref[...])
    bv_ref[...] = vn.reshape(bv_ref.shape)
    if single_row:
        mixed = vn * ws_ref[...] + bias_ref[...]
    else:
        row_i = lax.broadcasted_iota(I32, (CHUNK, CHUNK), 0)
        col_i = lax.broadcasted_iota(I32, (CHUNK, CHUNK), 1)
        lane = lax.broadcasted_iota(I32, (1, LANES), 1)
        vnb = vn.astype(BF16)
        parts = []
        for p in range(B_GROUPS // 2):
            vp = vnb[:, p * LANES:(p + 1) * LANES]
            w0 = jnp.where(row_i >= col_i, ws_ref[2 * p], 0.0).astype(BF16)
            w1 = jnp.where(row_i >= col_i, ws_ref[2 * p + 1], 0.0).astype(BF16)
            parts.append(jnp.where(lane < B_GDIM, _dot(w0, vp), _dot(w1, vp)))
        mixed = jnp.concatenate(parts, axis=1) + bias_ref[...]
    ob = ub * mixed * _silu(gb)
    oag = oa_ref[...] * _silu(ga)
    z = _dot(oag.astype(BF16), wout_ref[0:A_WIDTH, :]) + _dot(ob.astype(BF16), wout_ref[A_WIDTH:, :])
    y_ref[...] = _layer_norm(ALPHA * x_ref[...] + z, lng_ref[...], lnb_ref[...])


def _even_post(x2d, oa, rest, w_out, b_ws, b_bs, b_ln_g, b_ln_b, ln_g, ln_b, nb, single_row):
    n = x2d.shape[0]
    tm = CHUNK
    tpb = (n // nb) // tm if not single_row else 1
    row = lambda i: (i, 0)
    const2 = lambda i: (0, 0)
    if single_row:
        ws = jnp.repeat(b_ws[:, 0, 0], B_GDIM).reshape(1, B_WIDTH)
        bias = jnp.repeat(b_bs[:, 0], B_GDIM).reshape(1, B_WIDTH)
        ws_spec = pl.BlockSpec(ws.shape, const2)
        bv_shape = jax.ShapeDtypeStruct((n, B_WIDTH), F32)
        bv_spec = pl.BlockSpec((tm, B_WIDTH), row)
    else:
        ws = b_ws
        bias = jnp.repeat(b_bs.T, B_GDIM, axis=1)
        ws_spec = pl.BlockSpec(ws.shape, lambda i: (0, 0, 0))
        bv_shape = jax.ShapeDtypeStruct((nb, CHUNK, B_WIDTH), F32)
        bv_spec = pl.BlockSpec((1, CHUNK, B_WIDTH), lambda i: (i // tpb, 0, 0))
    in_specs = [
        pl.BlockSpec((tm, D_MODEL), row),
        pl.BlockSpec((tm, A_WIDTH), row),
        pl.BlockSpec((tm, 4 * A_WIDTH), row),
        pl.BlockSpec((A_WIDTH + B_WIDTH, D_MODEL), const2),
        ws_spec,
        pl.BlockSpec(bias.shape, const2),
        pl.BlockSpec((1, B_WIDTH), const2),
        pl.BlockSpec((1, B_WIDTH), const2),
        pl.BlockSpec((1, D_MODEL), const2),
        pl.BlockSpec((1, D_MODEL), const2),
    ]
    return pl.pallas_call(
        functools.partial(_even_post_kernel, single_row=single_row), grid=(n // tm,), in_specs=in_specs,
        out_specs=(pl.BlockSpec((tm, D_MODEL), row), bv_spec),
        out_shape=(jax.ShapeDtypeStruct((n, D_MODEL), F32), bv_shape),
        compiler_params=_params(("arbitrary",), 24 * 1024 * 1024), name="even_post",
    )(x2d, oa, rest, w_out.astype(BF16), ws, bias, b_ln_g.reshape(1, -1), b_ln_b.reshape(1, -1),
      ln_g.reshape(1, -1), ln_b.reshape(1, -1))


def _head_ones():
    r = lax.broadcasted_iota(I32, (LANES, LANES), 0) // C_HDIM
    c = lax.broadcasted_iota(I32, (LANES, LANES), 1) // C_HDIM
    return jnp.where(r == c, 1.0, 0.0).astype(BF16)


def _rwkv_pre_kernel(*refs, sample, has_vres, tpb):
    it = iter(refs)
    x_ref, xprev_ref, mu_ref, win_ref = next(it), next(it), next(it), next(it)
    w0_ref, w1_ref, w2_ref, a0_ref, a1_ref, a2_ref = (next(it) for _ in range(6))
    if has_vres:
        v0_ref, v1_ref, v2_ref, vfirst_ref = (next(it) for _ in range(4))
    kk_ref, ka_ref = next(it), next(it)
    r_ref, k_ref, v_ref, kkn_ref, a_ref, lw_ref, g_ref, vnat_ref = (next(it) for _ in range(8))

    x = x_ref[...]
    if sample:
        xp = xprev_ref[...]
    else:
        i = pl.program_id(0)
        prev_row = jnp.where(i % tpb == 0, 0.0, xprev_ref[SUBLANES - 1:SUBLANES, :])
        row_i = lax.broadcasted_iota(I32, x.shape, 0)
        xp = jnp.where(row_i == 0, prev_row, pltpu.roll(x, 1, axis=0))
    xx = xp - x
    mix = lambda j: (x + xx * mu_ref[j:j + 1, :])
    xr, xw, xk, xv, xa, xg = (mix(j) for j in range(6))
    r = _dot(xr.astype(BF16), win_ref[0])
    k = _dot(xk.astype(BF16), win_ref[1])
    v = _dot(xv.astype(BF16), win_ref[2])
    g = _dot(xg.astype(BF16), win_ref[3])
    z = w0_ref[...] + _dot(jnp.tanh(_dot(xw.astype(BF16), w1_ref[...])).astype(BF16), w2_ref[...])
    w_log = -(jnp.maximum(-z, 0.0) + jnp.log1p(jnp.exp(-jnp.abs(z)))) - 0.5
    lw = -jnp.exp(w_log)
    a = _sigmoid(a0_ref[...] + _dot(_dot(xa.astype(BF16), a1_ref[...]).astype(BF16), a2_ref[...]))
    if has_vres:
        gate = _sigmoid(v0_ref[...] + _dot(_dot(xv.astype(BF16), v1_ref[...]).astype(BF16), v2_ref[...]))
        v = v + (vfirst_ref[...] - v) * gate
    kk = k * kk_ref[...]
    ones_bd = _head_ones()
    ss = jnp.concatenate(
        [_dot_exact_rhs(jnp.square(kk[:, p * LANES:(p + 1) * LANES]), ones_bd) for p in range(N_PAIRS)], axis=1)
    kkn = kk / jnp.maximum(jnp.sqrt(ss), 1e-12)
    k2 = k * (1.0 + (a - 1.0) * ka_ref[...])
    g_ref[...] = g
    vnat_ref[...] = v
    outs = ((r_ref, r), (k_ref, k2), (v_ref, v), (kkn_ref, kkn), (a_ref, a), (lw_ref, lw))
    if sample:
        for ref, val in outs:
            ref[...] = val
    else:
        for ref, val in outs:
            for p in range(N_PAIRS):
                ref[0, p] = val[:, p * LANES:(p + 1) * LANES]


def _rwkv_pre(x2d, nb, sample, shift_prev, w, v_first):
    n = x2d.shape[0]
    t = n // nb
    tm = LANES if sample else min(128, t)
    tpb = max(t // tm, 1)
    has_vres = v_first is not None
    row = lambda i: (i, 0)
    const2 = lambda i: (0, 0)
    vec = lambda a: a.reshape(1, -1)
    args = [x2d]
    in_specs = [pl.BlockSpec((tm, D_MODEL), row)]
    if sample:
        args.append(shift_prev)
        in_specs.append(pl.BlockSpec((tm, D_MODEL), row))
    else:
        args.append(x2d)
        in_specs.append(pl.BlockSpec((SUBLANES, D_MODEL),
                                     lambda i: (jnp.maximum(i * (tm // SUBLANES) - 1, 0), 0)))
    args += [w["mu"], w["w_in"].astype(BF16), vec(w["w0"]), w["w1"].astype(BF16), w["w2"].astype(BF16),
             vec(w["a0"]), w["a1"].astype(BF16), w["a2"].astype(BF16)]
    in_specs += [pl.BlockSpec((6, D_MODEL), const2), pl.BlockSpec((4, D_MODEL, D_MODEL), lambda i: (0, 0, 0))]
    in_specs += [pl.BlockSpec(a.shape, const2) for a in args[4:]]
    if has_vres:
        extra = [vec(w["v0"]), w["v1"].astype(BF16), w["v2"].astype(BF16)]
        args += extra + [v_first]
        in_specs += [pl.BlockSpec(a.shape, const2) for a in extra] + [pl.BlockSpec((tm, D_MODEL), row)]
    args += [vec(w["kk"]), vec(w["ka"])]
    in_specs += [pl.BlockSpec((1, D_MODEL), const2)] * 2
    nat_shape = jax.ShapeDtypeStruct((n, D_MODEL), F32)
    nat_spec = pl.BlockSpec((tm, D_MODEL), row)
    if sample:
        scan_shape, scan_spec = nat_shape, nat_spec
    else:
        scan_shape = jax.ShapeDtypeStruct((nb, N_PAIRS, t, LANES), F32)
        scan_spec = pl.BlockSpec((1, N_PAIRS, tm, LANES), lambda i: (i // tpb, 0, i % tpb, 0))
    outs = pl.pallas_call(
        functools.partial(_rwkv_pre_kernel, sample=sample, has_vres=has_vres, tpb=tpb),
        grid=(n // tm,), in_specs=in_specs,
        out_specs=(scan_spec,) * 6 + (nat_spec,) * 2, out_shape=(scan_shape,) * 6 + (nat_shape,) * 2,
        compiler_params=_params(("arbitrary",), 56 * 1024 * 1024), name="rwkv_pre",
    )(*args)
    return outs[:6], outs[6], outs[7]


def _rwkv_scan_kernel(r_ref, k_ref, v_ref, kk_ref, a_ref, lw_ref, o_ref, sfin_ref, m_ref):
    c = pl.program_id(1)
    cs = RW_CHUNK
    rows = 2 * cs

    @pl.when(c == 0)
    def _():
        m_ref[...] = jnp.zeros(m_ref.shape, F32)

    ri = lax.broadcasted_iota(I32, (rows, rows), 0)
    ci = lax.broadcasted_iota(I32, (rows, rows), 1)
    same = (ri // cs) == (ci // cs)
    tr, tc = ri % cs, ci % cs
    strict = same & (tr > tc)
    incl = same & (tr >= tc)
    eye = jnp.where(ri == ci, 1.0, 0.0).astype(F32)
    tri = jnp.where(lax.broadcasted_iota(I32, (cs, cs), 0) >= lax.broadcasted_iota(I32, (cs, cs), 1),
                    1.0, 0.0).astype(BF16)
    head0 = lax.broadcasted_iota(I32, (1, LANES), 1) < C_HDIM

    def stack(y):
        return jnp.concatenate([jnp.where(head0, y, 0.0), jnp.where(head0, 0.0, y)], axis=0)

    def off_diag(s):
        return same & ((tr & s) != 0) & ((tc & s) == 0) & ((tr // (2 * s)) == (tc // (2 * s)))

    def one_pair(p):
        r, k, v = r_ref[0, p], k_ref[0, p], v_ref[0, p]
        kk, a, lw = kk_ref[0, p], a_ref[0, p], lw_ref[0, p]
        cum = _dot_exact_lhs(tri, lw)
        yield
        tot = cum[cs - 1:cs, :]
        b = kk * a
        e_neg = jnp.exp(-cum)
        e_tail = jnp.exp(tot - cum)
        kap_s = stack(kk * jnp.exp(cum - lw))
        rt_s = stack(r * jnp.exp(cum))
        vb = stack(v).astype(BF16)
        left = jnp.concatenate([kap_s, rt_s], axis=0).astype(BF16)
        right = jnp.concatenate([stack(b * e_neg), stack(k * e_neg)], axis=0).astype(BF16)
        gram = _dot_nt(left, right)
        yield
        l_mat = jnp.where(strict, gram[:rows, :rows], 0.0)
        akk = jnp.where(strict, gram[:rows, rows:], 0.0).astype(BF16)
        arb = jnp.where(incl, gram[rows:, :rows], 0.0).astype(BF16)
        ark = jnp.where(incl, gram[rows:, rows:], 0.0).astype(BF16)
        tinv = eye - jnp.where(off_diag(1), l_mat, 0.0)
        akkv = _dot(akk, vb)
        s = 2
        while s < cs:
            tb = tinv.astype(BF16)
            inner = _dot(jnp.where(off_diag(s), l_mat, 0.0).astype(BF16), tb)
            yield
            tinv = tinv - _dot(tb, inner.astype(BF16))
            yield
            s *= 2
        sol = _dot(tinv.astype(BF16), jnp.concatenate([kap_s, akkv], axis=1).astype(BF16))
        yield
        kpb, u0b = sol[:, :LANES].astype(BF16), (-sol[:, LANES:]).astype(BF16)
        rp = rt_s - _dot(arb, kpb)
        o0 = _dot(arb, u0b) + _dot(ark, vb)
        bht = stack(b * e_tail).T.astype(BF16)
        kht = stack(k * e_tail).T.astype(BF16)
        trans = jnp.where(ri == ci, jnp.exp(tot), 0.0) - _dot(bht, kpb)
        gain = _dot(bht, u0b) + _dot(kht, vb)
        yield
        m2 = _split2(m_ref[p])
        o = _mm3(_split2(rp), m2) + o0
        o_ref[0, p] = o[:cs] + o[cs:]
        m_ref[p] = _mm3(_split2(trans), m2) + gain

    chains = [one_pair(p) for p in range(N_PAIRS)]
    while chains:
        chains = [c for c in chains if next(c, "done") != "done"]

    @pl.when(c == pl.num_programs(1) - 1)
    def _():
        sfin_ref[0] = m_ref[...]


def _rwkv_scan(scan_in, nb):
    t = scan_in[0].shape[2]
    nc = t // RW_CHUNK
    blk = pl.BlockSpec((1, N_PAIRS, RW_CHUNK, LANES), lambda b, c: (b, 0, c, 0))
    return pl.pallas_call(
        _rwkv_scan_kernel, grid=(nb, nc), in_specs=[blk] * 6,
        out_specs=(blk, pl.BlockSpec((1, N_PAIRS, LANES, LANES), lambda b, c: (b, 0, 0, 0))),
        out_shape=(jax.ShapeDtypeStruct((nb, N_PAIRS, t, LANES), F32),
                   jax.ShapeDtypeStruct((nb, N_PAIRS, LANES, LANES), F32)),
        scratch_shapes=[pltpu.VMEM((N_PAIRS, LANES, LANES), F32)],
        compiler_params=_params(("arbitrary", "arbitrary"), 32 * 1024 * 1024), name="rwkv_scan",
    )(*scan_in)


def _pair_state_to_heads(m):
    nb = m.shape[0]
    m = m.reshape(nb, N_PAIRS, 2, C_HDIM, 2, C_HDIM)
    heads = jnp.stack([m[:, :, 0, :, 0, :], m[:, :, 1, :, 1, :]], axis=2)
    return jnp.swapaxes(heads.reshape(nb, C_HEADS, C_HDIM, C_HDIM), -1, -2)


def _rwkv_step_kernel(s_ref, r_ref, k_ref, v_ref, kk_ref, a_ref, lw_ref, snew_ref, o_ref):
    r, k, v = r_ref[...], k_ref[...], v_ref[...]
    kk, a = kk_ref[...], a_ref[...]
    w = jnp.exp(lw_ref[...])
    nkk = -kk
    b = kk * a
    eye = (lax.broadcasted_iota(I32, (C_HDIM, C_HDIM), 0) == lax.broadcasted_iota(I32, (C_HDIM, C_HDIM), 1))
    heads = range(C_HEADS)
    lanes = [slice(h * C_HDIM, (h + 1) * C_HDIM) for h in heads]
    s = [s_ref[h] for h in heads]
    sa = [jnp.sum(s[h] * nkk[:, lanes[h]], axis=1, keepdims=True) for h in heads]
    v_col = [jnp.sum(jnp.where(eye, v[:, lanes[h]], 0.0), axis=1, keepdims=True) for h in heads]
    s_n = [s[h] * w[:, lanes[h]] + sa[h] * b[:, lanes[h]] + v_col[h] * k[:, lanes[h]] for h in heads]
    for h in heads:
        snew_ref[h] = s_n[h]
    out_col = [jnp.sum(s_n[h] * r[:, lanes[h]], axis=1, keepdims=True) for h in heads]
    out_rows = [jnp.sum(jnp.where(eye, out_col[h], 0.0), axis=0, keepdims=True) for h in heads]
    o_ref[...] = jnp.concatenate(out_rows, axis=1)


def _rwkv_step(states, o, scan_in):
    db = states.shape[1]
    vec = pl.BlockSpec((None, 1, D_MODEL), lambda b: (b, 0, 0))
    st_in = pl.BlockSpec((None, None, C_HEADS, C_HDIM, C_HDIM), lambda b: (o, b, 0, 0, 0))
    st = pl.BlockSpec((None, C_HEADS, C_HDIM, C_HDIM), lambda b: (b, 0, 0, 0))
    new, out = pl.pallas_call(
        _rwkv_step_kernel, grid=(db,), in_specs=[st_in] + [vec] * 6, out_specs=(st, vec),
        out_shape=(jax.ShapeDtypeStruct(states.shape[1:], F32), jax.ShapeDtypeStruct((db, 1, D_MODEL), F32)),
        compiler_params=_params(("arbitrary",), 16 * 1024 * 1024), name="rwkv_step",
    )(states, *[a.reshape(db, 1, D_MODEL) for a in scan_in])
    return new, out.reshape(db, D_MODEL)


def _rwkv_post_kernel(x_ref, o_ref, r_ref, k_ref, v_ref, g_ref, rk_ref, lnxg_ref, lnxb_ref, wout_ref,
                      lng_ref, lnb_ref, y_ref, *, sample):
    if sample:
        get = lambda p: tuple(ref[:, p * LANES:(p + 1) * LANES] for ref in (o_ref, r_ref, k_ref, v_ref))
    else:
        get = lambda p: (o_ref[0, p], r_ref[0, p], k_ref[0, p], v_ref[0, p])
    ones_bd = _head_ones()
    parts = []
    for p in range(N_PAIRS):
        o, r, k, v = get(p)
        lanes = slice(p * LANES, (p + 1) * LANES)
        mean = _dot_exact_rhs(o, ones_bd) * (1.0 / C_HDIM)
        d = o - mean
        var = _dot_exact_rhs(d * d, ones_bd) * (1.0 / C_HDIM)
        on = d * lax.rsqrt(var + GN_EPS) * lnxg_ref[:, lanes] + lnxb_ref[:, lanes]
        bonus = _dot_exact_rhs(r * k * rk_ref[:, lanes], ones_bd) * v
        parts.append(on + bonus)
    out = jnp.concatenate(parts, axis=1) * _silu(g_ref[...])
    z = _dot(out.astype(BF16), wout_ref[...])
    y_ref[...] = _layer_norm(ALPHA * x_ref[...] + z, lng_ref[...], lnb_ref[...])


def _rwkv_post(x2d, o, scan_in, g, w, ln_g, ln_b, nb, sample):
    n = x2d.shape[0]
    t = n // nb
    tm = LANES if sample else min(256, t)
    tpb = max(t // tm, 1)
    row = lambda i: (i, 0)
    const2 = lambda i: (0, 0)
    vec = lambda a: a.reshape(1, -1)
    if sample:
        scan_spec = pl.BlockSpec((tm, D_MODEL), row)
    else:
        scan_spec = pl.BlockSpec((1, N_PAIRS, tm, LANES), lambda i: (i // tpb, 0, i % tpb, 0))
    r, k, v = scan_in[0], scan_in[1], scan_in[2]
    in_specs = [pl.BlockSpec((tm, D_MODEL), row), scan_spec, scan_spec, scan_spec, scan_spec,
                pl.BlockSpec((tm, D_MODEL), row)]
    in_specs += [pl.BlockSpec((1, D_MODEL), const2)] * 3 + [pl.BlockSpec((D_MODEL, D_MODEL), const2)]
    in_specs += [pl.BlockSpec((1, D_MODEL), const2)] * 2
    return pl.pallas_call(
        functools.partial(_rwkv_post_kernel, sample=sample), grid=(n // tm,), in_specs=in_specs,
        out_specs=pl.BlockSpec((tm, D_MODEL), row), out_shape=jax.ShapeDtypeStruct((n, D_MODEL), F32),
        compiler_params=_params(("arbitrary",), 32 * 1024 * 1024), name="rwkv_post",
    )(x2d, o, r, k, v, g, vec(w["rk"]), vec(w["lnx_g"]), vec(w["lnx_b"]), w["w_out"].astype(BF16),
      vec(ln_g), vec(ln_b))


def kernel(x_prompt, x_sample, cache_k, cache_v, cache_idx_k, state_wkv, state_shift, page_table, ln_g, ln_b, e_w_in, e_w_out, b_ws, b_bs, b_ln_g, b_ln_b, c_mu, c_w_in, c_w0, c_w1, c_w2, c_a0, c_a1, c_a2, c_v0, c_v1, c_v2, c_kk, c_ka, c_rk, c_lnx_g, c_lnx_b, c_w_out):
    bp, seq, _ = x_prompt.shape
    db, dec_seq, _ = x_sample.shape
    assert dec_seq == 1 and db == LANES and seq % KT == 0
    n_pages = page_table.shape[1]
    past = n_pages * PAGE_SIZE

    def odd_weights(o):
        w = dict(mu=c_mu[o], w_in=c_w_in[o], w0=c_w0[o], w1=c_w1[o], w2=c_w2[o], a0=c_a0[o], a1=c_a1[o],
                 a2=c_a2[o], kk=c_kk[o], ka=c_ka[o], rk=c_rk[o], lnx_g=c_lnx_g[o], lnx_b=c_lnx_b[o],
                 w_out=c_w_out[o])
        if o > 0:
            w.update(v0=c_v0[o - 1], v1=c_v1[o - 1], v2=c_v2[o - 1])
        return w

    x = x_prompt.reshape(bp * seq, D_MODEL)
    p_k, p_v, p_ik, p_bv, p_wkv, p_shift = [], [], [], [], [], []
    v_first = None
    for l in range(DEPTH):
        if l % 2 == 0:
            e = l // 2
            qs, k, kb, v, vt, qi, ki, kib, wit, rest = _even_proj(x, e_w_in[e], bp)
            oa = _attn_prompt(qs, qi, wit, kb, vt, kib, bp)
            x, bv = _even_post(x, oa, rest, e_w_out[e], b_ws[e], b_bs[e], b_ln_g[e], b_ln_b[e],
                               ln_g[l], ln_b[l], bp, single_row=False)
            p_k.append(k.reshape(bp, seq, A_HEADS, HEAD_DIM))
            p_v.append(v.reshape(bp, seq, A_HEADS, HEAD_DIM))
            p_ik.append(ki.reshape(bp, seq, IDX_DIM))
            p_bv.append(bv)
        else:
            o = l // 2
            w = odd_weights(o)
            p_shift.append(x.reshape(bp, seq, D_MODEL)[:, -1])
            scan_in, g, v_nat = _rwkv_pre(x, bp, False, None, w, v_first)
            if o == 0:
                v_first = v_nat
            out, m_fin = _rwkv_scan(scan_in, bp)
            p_wkv.append(_pair_state_to_heads(m_fin))
            x = _rwkv_post(x, out, scan_in, g, w, ln_g[l], ln_b[l], bp, sample=False)
    y_prompt = x.reshape(bp, seq, D_MODEL)

    x = x_sample.reshape(db, D_MODEL)
    s_k, s_v, s_ik, s_bv, s_wkv, s_shift = [], [], [], [], [], []
    v_first = None
    for l in range(DEPTH):
        if l % 2 == 0:
            e = l // 2
            qs, k, kb, v, vt, qi, ki, kib, wit, rest = _even_proj(x, e_w_in[e], 1)
            keys = _sample_scores(page_table, qi.astype(BF16), wit[0].T, ki, cache_idx_k, e)
            thr, cut = _sample_select(keys, past + 1)
            oa = _sample_attn(page_table, thr, cut, qs, k, v, keys, cache_k, cache_v, e)
            x, bv = _even_post(x, oa, rest, e_w_out[e], b_ws[e], b_bs[e], b_ln_g[e], b_ln_b[e],
                               ln_g[l], ln_b[l], 1, single_row=True)
            s_k.append(k.reshape(db, 1, A_HEADS, HEAD_DIM))
            s_v.append(v.reshape(db, 1, A_HEADS, HEAD_DIM))
            s_ik.append(ki.reshape(db, 1, IDX_DIM))
            s_bv.append(bv.reshape(db, 1, B_WIDTH))
        else:
            o = l // 2
            w = odd_weights(o)
            s_shift.append(x)
            scan_in, g, v_nat = _rwkv_pre(x, 1, True, state_shift[o], w, v_first)
            if o == 0:
                v_first = v_nat
            new_state, out = _rwkv_step(state_wkv, o, scan_in)
            s_wkv.append(new_state)
            x = _rwkv_post(x, out, scan_in, g, w, ln_g[l], ln_b[l], 1, sample=True)
    y_sample = x.reshape(db, 1, D_MODEL)

    return (y_prompt, y_sample, jnp.stack(p_k), jnp.stack(p_v), jnp.stack(p_ik), jnp.stack(p_bv),
            jnp.stack(p_wkv), jnp.stack(p_shift), jnp.stack(s_k), jnp.stack(s_v), jnp.stack(s_ik),
            jnp.stack(s_bv), jnp.stack(s_wkv), jnp.stack(s_shift))
```
